```python
import math
import jax, jax.numpy as jnp
from jax import lax
import numpy as np


D_MODEL = 1024
BATCH = 4
SEQ = 8192
DEPTH = 1

ML_WIDTH = D_MODEL
ML_HEADS = 4
ML_HEAD_DIM = ML_WIDTH // ML_HEADS
ML_CHUNK = 128
CONV_WIDTH = 5
DA_WIDTH = D_MODEL
DA_HEADS = 8
DA_V_DIM = DA_WIDTH // DA_HEADS
DA_QK_DIM = DA_V_DIM // 2
Q_BLOCK = 128
ROPE_THETA = 10000.0
NORM_EPS = 1e-6
SEG_SIZES = (ML_WIDTH,) * 5 + (4 * ML_HEADS,) + (DA_WIDTH,) * 4 + (D_MODEL,) * 2
PROJ_WIDTH = sum(SEG_SIZES)

kernel_name = "hybrid_mlstm_diffattn_gated_block"


def _split_points():
    pts, acc = [], 0
    for s in SEG_SIZES[:-1]:
        acc += s
        pts.append(acc)
    return tuple(pts)


def rmsnorm(x, g):
    xf = x.astype(jnp.float32)
    y = xf * lax.rsqrt(jnp.mean(xf * xf, axis=-1, keepdims=True) + NORM_EPS)
    return (y * g.astype(jnp.float32)).astype(x.dtype)


def centred_depthwise_conv(x, w):
    pad = CONV_WIDTH // 2
    return lax.conv_general_dilated(
        x, w[:, None, :].astype(x.dtype), window_strides=(1,), padding=[(pad, pad)],
        dimension_numbers=('NWC', 'WIO', 'NWC'), feature_group_count=x.shape[-1])


def apply_rope(t, positions):
    d = t.shape[-1]
    inv_freq = ROPE_THETA ** (-jnp.arange(0, d, 2, dtype=jnp.float32) / d)
    ang = positions.astype(jnp.float32)[..., None] * inv_freq
    cos = jnp.cos(ang)[:, :, None, None, :]
    sin = jnp.sin(ang)[:, :, None, None, :]
    tf = t.astype(jnp.float32)
    t1, t2 = tf[..., : d // 2], tf[..., d // 2:]
    return jnp.concatenate([t1 * cos - t2 * sin, t2 * cos + t1 * sin], axis=-1)


def mlstm_one_direction(q, k, v, i_pre, log_f):
    B, H, S, d = q.shape
    L = ML_CHUNK
    nc = S // L

    def chunks(t):
        return jnp.moveaxis(t.reshape(t.shape[:2] + (nc, L) + t.shape[3:]), 2, 0)

    lower = jnp.tril(jnp.ones((L, L), dtype=bool))

    def step(carry, inp):
        c_mat, n_vec, m_st = carry
        qc, kc, vc, ic, fc = inp
        b = jnp.cumsum(fc, axis=-1)
        log_d = jnp.where(lower, b[..., :, None] - b[..., None, :] + ic[..., None, :], -jnp.inf)
        log_inter = b + m_st[..., None]
        m_t = jnp.maximum(log_inter, jnp.max(log_d, axis=-1))
        w_intra = jnp.einsum('bhtd,bhsd->bhts', qc, kc) * jnp.exp(log_d - m_t[..., None])
        s_inter = jnp.exp(log_inter - m_t)
        num = (jnp.einsum('bhts,bhsd->bhtd', w_intra, vc)
               + s_inter[..., None] * jnp.einsum('bhvk,bhtk->bhtv', c_mat, qc))
        den = jnp.sum(w_intra, axis=-1) + s_inter * jnp.einsum('bhk,bhtk->bht', n_vec, qc)
        h = num / jnp.maximum(jnp.abs(den), jnp.exp(-m_t))[..., None]
        b_last = b[..., -1]
        log_w = b_last[..., None] - b + ic
        m_new = jnp.maximum(b_last + m_st, jnp.max(log_w, axis=-1))
        w_state = jnp.exp(log_w - m_new[..., None])
        decay = jnp.exp(b_last + m_st - m_new)
        c_new = decay[..., None, None] * c_mat + jnp.einsum('bhsv,bhsk->bhvk', w_state[..., None] * vc, kc)
        n_new = decay[..., None] * n_vec + jnp.einsum('bhs,bhsk->bhk', w_state, kc)
        return (c_new, n_new, m_new), h

    init = (jnp.zeros((B, H, d, d), jnp.float32), jnp.zeros((B, H, d), jnp.float32),
            jnp.zeros((B, H), jnp.float32))
    _, h = lax.scan(step, init, (chunks(q), chunks(k), chunks(v), chunks(i_pre), chunks(log_f)))
    return jnp.moveaxis(h, 0, 2).reshape(B, H, S, d)


def diff_attention(q, k, v, lam):
    B, S, H = q.shape[:3]
    nb = S // Q_BLOCK
    q_blocks = jnp.moveaxis(q.reshape((B, nb, Q_BLOCK) + q.shape[2:]), 1, 0)
    scale = DA_QK_DIM ** -0.5

    def one_block(qb):
        scores = jnp.einsum('bqhjd,bkhjd->bhjqk', qb, k) * scale
        p = jax.nn.softmax(scores, axis=-1)
        p_diff = p[:, :, 0] - lam * p[:, :, 1]
        return jnp.einsum('bhqk,bkhd->bqhd', p_diff, v)

    out = lax.map(one_block, q_blocks)
    return jnp.moveaxis(out, 0, 1).reshape(B, S, H, v.shape[-1])


def setup_inputs(seed: int = 0) -> dict:
    key = jax.random.key(seed)
    ks = jax.random.split(key, 14)
    f32 = jnp.float32
    x = jax.random.normal(ks[0], (BATCH, SEQ, D_MODEL), f32)
    positions = (jnp.arange(SEQ, dtype=jnp.int32)[None, :]
                 + jax.random.randint(ks[1], (BATCH, 1), 0, 1024, dtype=jnp.int32))
    norm_g = 1.0 + 0.02 * jax.random.normal(ks[2], (DEPTH, D_MODEL), f32)
    w_in = jax.random.normal(ks[3], (DEPTH, D_MODEL, PROJ_WIDTH), f32) * D_MODEL ** -0.5
    zeros_h = jnp.zeros((ML_HEADS,), f32)
    f_off = jnp.linspace(3.0, 6.0, ML_HEADS, dtype=f32)
    gate_offsets = jnp.stack([zeros_h, f_off, zeros_h, f_off])
    ml_gate_b = 0.1 * jax.random.normal(ks[4], (DEPTH, 4, ML_HEADS), f32) + gate_offsets[None]
    ml_conv_w = jax.random.normal(ks[5], (DEPTH, CONV_WIDTH, 2 * ML_WIDTH), f32) * CONV_WIDTH ** -0.5
    ml_norm_g = 1.0 + 0.02 * jax.random.normal(ks[6], (DEPTH, ML_HEADS, ML_HEAD_DIM), f32)
    da_lambda = 0.1 * jax.random.normal(ks[7], (DEPTH, 4, DA_QK_DIM), f32)
    da_subln_g = 1.0 + 0.02 * jax.random.normal(ks[8], (DEPTH, DA_V_DIM), f32)
    gate_b = 0.02 * jax.random.normal(ks[9], (DEPTH, 2, D_MODEL), f32)
    w_branch_a = jax.random.normal(ks[10], (DEPTH, ML_WIDTH, D_MODEL), f32) * ML_WIDTH ** -0.5
    w_branch_b = jax.random.normal(ks[11], (DEPTH, DA_WIDTH, D_MODEL), f32) * DA_WIDTH ** -0.5
    w_out = jax.random.normal(ks[12], (DEPTH, D_MODEL, D_MODEL), f32) * D_MODEL ** -0.5
    final_g = 1.0 + 0.02 * jax.random.normal(ks[13], (D_MODEL,), f32)
    return {"x": x, "positions": positions, "norm_g": norm_g, "w_in": w_in,
            "ml_gate_b": ml_gate_b, "ml_conv_w": ml_conv_w, "ml_norm_g": ml_norm_g,
            "da_lambda": da_lambda, "da_subln_g": da_subln_g, "gate_b": gate_b,
            "w_branch_a": w_branch_a, "w_branch_b": w_branch_b, "w_out": w_out,
            "final_g": final_g}


def reference(x, positions, norm_g, w_in, ml_gate_b, ml_conv_w, ml_norm_g, da_lambda,
              da_subln_g, gate_b, w_branch_a, w_branch_b, w_out, final_g):
    B, S, _ = x.shape
    f32 = jnp.float32

    def heads(t):
        return t.reshape(B, S, ML_HEADS, ML_HEAD_DIM).transpose(0, 2, 1, 3).astype(f32)

    def flip(t):
        return jnp.flip(t, axis=2)

    for layer in range(DEPTH):
        h = rmsnorm(x, norm_g[layer])
        proj = jnp.einsum('bsd,de->bse', h, w_in[layer])
        (a_q, a_k, a_v, a_o, a_z, a_g, b_q, b_k, b_v, b_z, g_a, g_b) = jnp.split(
            proj, _split_points(), axis=-1)

        qk = jax.nn.silu(centred_depthwise_conv(jnp.concatenate([a_q, a_k], axis=-1), ml_conv_w[layer]))
        mq = heads(qk[..., :ML_WIDTH])
        mk = heads(qk[..., ML_WIDTH:]) * (ML_HEAD_DIM ** -0.5)
        mv = heads(a_v)
        gates = (a_g.astype(f32).reshape(B, S, 4, ML_HEADS)
                 + ml_gate_b[layer].astype(f32)).transpose(2, 0, 3, 1)
        i_fw, lf_fw = gates[0], jax.nn.log_sigmoid(gates[1])
        i_bw, lf_bw = gates[2], jax.nn.log_sigmoid(gates[3])
        h_fw = mlstm_one_direction(mq, mk, mv, i_fw, lf_fw)
        h_bw = flip(mlstm_one_direction(flip(mq), flip(mk), flip(mv), flip(i_bw), flip(lf_bw)))
        hm = rmsnorm((h_fw + h_bw).transpose(0, 2, 1, 3), ml_norm_g[layer])
        hm = hm.reshape(B, S, ML_WIDTH).astype(x.dtype)
        y_a = hm * jax.nn.sigmoid(a_o) * jax.nn.silu(a_z)

        lambda_init = 0.8 - 0.6 * math.exp(-0.3 * layer)
        lp = da_lambda[layer].astype(f32)
        lam = jnp.exp(jnp.sum(lp[0] * lp[1])) - jnp.exp(jnp.sum(lp[2] * lp[3])) + lambda_init
        dq = apply_rope(b_q.reshape(B, S, DA_HEADS, 2, DA_QK_DIM), positions)
        dk = apply_rope(b_k.reshape(B, S, DA_HEADS, 2, DA_QK_DIM), positions)
        da_v = b_v.reshape(B, S, DA_HEADS, DA_V_DIM).astype(f32)
        att = diff_attention(dq, dk, da_v, lam)
        att = rmsnorm(att, da_subln_g[layer]) * (1.0 - lambda_init)
        y_b = att.reshape(B, S, DA_WIDTH).astype(x.dtype) * jax.nn.silu(b_z)

        gb = gate_b[layer]
        mix = (jax.nn.sigmoid(g_a + gb[0]) * jnp.einsum('bsc,cd->bsd', y_a, w_branch_a[layer])
               + jax.nn.sigmoid(g_b + gb[1]) * jnp.einsum('bsc,cd->bsd', y_b, w_branch_b[layer]))
        x = x + jnp.einsum('bsd,de->bse', mix, w_out[layer])

    return rmsnorm(x, final_g)
```

```python
import functools
import math

import jax
import jax.numpy as jnp
from jax import lax
from jax.experimental import pallas as pl
from jax.experimental.pallas import tpu as pltpu

F32 = jnp.float32
BF16 = jnp.bfloat16

NORM_EPS = 1e-6
ROPE_THETA = 10000.0
ML_CHUNK = 128
LANES = 128
SUBLANES = 8
VMEM_LIMIT_BYTES = 56 * 1024 * 1024

_NT = (((1,), (1,)), ((), ()))
_TN = (((0,), (0,)), ((), ()))


def _params(n_axes):
    return pltpu.CompilerParams(
        dimension_semantics=("arbitrary",) * n_axes,
        vmem_limit_bytes=VMEM_LIMIT_BYTES)


def _sigmoid(x):
    return 1.0 / (1.0 + jnp.exp(-x))


def _silu(x):
    return x * _sigmoid(x)


def _log_sigmoid(x):
    return jnp.minimum(x, 0.0) - jnp.log1p(jnp.exp(-jnp.abs(x)))


def _in_proj_body(x_ref, g_ref, w_ref, wg_ref, wgt_ref, p_ref, gates_ref, gatest_ref, h_scr):
    @pl.when(pl.program_id(1) == 0)
    def _():
        x = x_ref[...]
        ms = jnp.mean(x * x, axis=-1, keepdims=True)
        h = (x * lax.rsqrt(ms + NORM_EPS) * g_ref[...]).astype(BF16)
        h_scr[...] = h
        gates_ref[...] = jnp.dot(h, wg_ref[...], preferred_element_type=F32)
        gatest_ref[...] = lax.dot_general(wgt_ref[...], h, _NT, preferred_element_type=F32)

    p_ref[...] = jnp.dot(h_scr[...], w_ref[...], preferred_element_type=F32).astype(BF16)


def _in_proj(x2, norm_g, w_main, wg, wgt, tm, tn):
    n, d = x2.shape
    width = w_main.shape[1]
    ng = wgt.shape[0]
    return pl.pallas_call(
        _in_proj_body,
        grid=(n // tm, width // tn),
        in_specs=[
            pl.BlockSpec((tm, d), lambda i, j: (i, 0)),
            pl.BlockSpec((1, d), lambda i, j: (0, 0)),
            pl.BlockSpec((d, tn), lambda i, j: (0, j)),
            pl.BlockSpec((d, LANES), lambda i, j: (0, 0)),
            pl.BlockSpec((ng, d), lambda i, j: (0, 0)),
        ],
        out_specs=[
            pl.BlockSpec((tm, tn), lambda i, j: (i, j)),
            pl.BlockSpec((tm, LANES), lambda i, j: (i, 0)),
            pl.BlockSpec((ng, tm), lambda i, j: (0, i)),
        ],
        out_shape=[
            jax.ShapeDtypeStruct((n, width), BF16),
            jax.ShapeDtypeStruct((n, LANES), F32),
            jax.ShapeDtypeStruct((ng, n), F32),
        ],
        scratch_shapes=[pltpu.VMEM((tm, d), BF16)],
        compiler_params=_params(2),
        name="in_proj",
    )(x2, norm_g, w_main, wg, wgt)


def _prep_body(qk_ref, prev_ref, next_ref, bq_ref, bk_ref, pos_ref, freq_ref, cw_ref,
               qk_out, rq_out, rk_out, ext_scr, *, tiles_per_seq, conv_width, k_scale, q_scale,
               n_heads):
    i = pl.program_id(0)
    tm = qk_ref.shape[0]
    width = qk_ref.shape[1]
    pad = conv_width // 2
    halo = SUBLANES
    t_in_seq = i % tiles_per_seq
    has_prev = (t_in_seq > 0).astype(F32)
    has_next = (t_in_seq < tiles_per_seq - 1).astype(F32)

    slab = ext_scr.shape[1]
    for c0 in range(0, width, slab):
        cols = slice(c0, c0 + slab)
        ext_scr[0:halo, :] = prev_ref[:, cols].astype(F32) * has_prev
        ext_scr[halo:halo + tm, :] = qk_ref[:, cols].astype(F32)
        ext_scr[halo + tm:halo + tm + halo, :] = next_ref[:, cols].astype(F32) * has_next
        acc = None
        for t in range(conv_width):
            term = ext_scr[pl.ds(halo - pad + t, tm), :] * cw_ref[t:t + 1, cols]
            acc = term if acc is None else acc + term
        y = _silu(acc)
        scale = 1.0 if c0 < width // 2 else k_scale
        qk_out[:, cols] = (y * scale).astype(BF16)

    ang = pos_ref[...].astype(F32) * freq_ref[...]
    cos = jnp.cos(ang)
    sin = jnp.sin(ang)
    lane = lax.broadcasted_iota(jnp.int32, (tm, LANES), 1)
    d_map = LANES // 2
    low = (lane % d_map) < (d_map // 2)
    sin_signed = jnp.where(low, -sin, sin)
    for src, dst, scale in ((bq_ref, rq_out, q_scale), (bk_ref, rk_out, 1.0)):
        for h in range(n_heads):
            cols = slice(h * LANES, (h + 1) * LANES)
            t = src[:, cols].astype(F32)
            up = pltpu.roll(t, LANES - d_map // 2, 1)
            down = pltpu.roll(t, d_map // 2, 1)
            rot = jnp.where(low, up, down)
            dst[:, cols] = ((t * cos + rot * sin_signed) * scale).astype(BF16)


def _prep(p, pos2, freq_tile, conv_w, *, seq, ml_width, da_width, bq_col, tm, k_scale, q_scale):
    n = p.shape[0]
    n_tiles = n // tm
    halo_blocks = n // SUBLANES
    qk_w = 2 * ml_width
    body = functools.partial(
        _prep_body, tiles_per_seq=seq // tm, conv_width=conv_w.shape[0], k_scale=k_scale,
        q_scale=q_scale, n_heads=da_width // LANES)
    per_tile = tm // SUBLANES
    return pl.pallas_call(
        body,
        grid=(n_tiles,),
        in_specs=[
            pl.BlockSpec((tm, qk_w), lambda i: (i, 0)),
            pl.BlockSpec((SUBLANES, qk_w), lambda i: (jnp.maximum(i * per_tile - 1, 0), 0)),
            pl.BlockSpec((SUBLANES, qk_w),
                         lambda i: (jnp.minimum((i + 1) * per_tile, halo_blocks - 1), 0)),
            pl.BlockSpec((tm, da_width), lambda i: (i, bq_col)),
            pl.BlockSpec((tm, da_width), lambda i: (i, bq_col + 1)),
            pl.BlockSpec((tm, 1), lambda i: (i, 0)),
            pl.BlockSpec((1, LANES), lambda i: (0, 0)),
            pl.BlockSpec((conv_w.shape[0], qk_w), lambda i: (0, 0)),
        ],
        out_specs=[
            pl.BlockSpec((tm, qk_w), lambda i: (i, 0)),
            pl.BlockSpec((tm, da_width), lambda i: (i, 0)),
            pl.BlockSpec((tm, da_width), lambda i: (i, 0)),
        ],
        out_shape=[
            jax.ShapeDtypeStruct((n, qk_w), BF16),
            jax.ShapeDtypeStruct((n, da_width), BF16),
            jax.ShapeDtypeStruct((n, da_width), BF16),
        ],
        scratch_shapes=[pltpu.VMEM((tm + 2 * SUBLANES, 256), F32)],
        compiler_params=_params(1),
        name="prep",
    )(p, p, p, p, p, pos2, freq_tile, conv_w)


def _split3(f):
    hi = f.astype(BF16)
    r1 = f - hi.astype(F32)
    mid = r1.astype(BF16)
    lo = (r1 - mid.astype(F32)).astype(BF16)
    return hi, mid, lo


def _mlstm_chunk(qc, kc, vc, i_row, f_row, i_col, f_col, c_ref, n_ref, m_ref, forward):
    L = qc.shape[0]
    r = lax.broadcasted_iota(jnp.int32, (L, L), 0)
    c = lax.broadcasted_iota(jnp.int32, (L, L), 1)
    valid = (c <= r) if forward else (c >= r)
    tri_left = valid.astype(BF16)
    tri_right = ((r <= c) if forward else (r >= c)).astype(BF16)

    f_b = jnp.broadcast_to(f_col, (L, LANES))
    b_col = sum(jnp.dot(tri_left, part, preferred_element_type=F32) for part in _split3(f_b))
    f_r = jnp.broadcast_to(f_row, (SUBLANES, L))
    b_row = sum(jnp.dot(part, tri_right, preferred_element_type=F32) for part in _split3(f_r))
    b_c = b_col[:, 0:1]
    a_row = i_row - b_row[0:1, :]
    a_col = i_col - b_c

    m_st = m_ref[...]
    log_d = jnp.where(valid, b_col + a_row, -jnp.inf)
    m_t = jnp.maximum(b_c + m_st, jnp.max(log_d, axis=-1, keepdims=True))
    s_qk = lax.dot_general(qc, kc, _NT, preferred_element_type=F32)
    w_intra = s_qk * jnp.exp(log_d - m_t)
    s_inter = jnp.exp(b_c + m_st - m_t)
    c_mat = c_ref[...]
    inter = lax.dot_general(qc, c_mat.astype(BF16), _NT, preferred_element_type=F32)
    num = jnp.dot(w_intra.astype(BF16), vc, preferred_element_type=F32) + s_inter * inter
    n_vec = n_ref[...]
    qn = jnp.sum(qc.astype(F32) * n_vec, axis=-1, keepdims=True)
    den = jnp.sum(w_intra, axis=-1, keepdims=True) + s_inter * qn
    h = num / jnp.maximum(jnp.abs(den), jnp.exp(-m_t))

    b_last = b_c[L - 1:L, :] if forward else b_c[0:1, :]
    m_new = jnp.maximum(b_last + m_st, jnp.max(b_last + a_row, axis=-1, keepdims=True))
    w_col = jnp.exp(b_last + a_col - m_new)
    decay = jnp.exp(b_last + m_st - m_new)
    wv = (w_col * vc.astype(F32)).astype(BF16)
    c_ref[...] = decay * c_mat + lax.dot_general(wv, kc, _TN, preferred_element_type=F32)
    n_ref[...] = decay * n_vec + jnp.sum(w_col * kc.astype(F32), axis=0, keepdims=True)
    m_ref[...] = m_new
    return h


def _mlstm_body(qf_ref, kf_ref, vf_ref, gf_ref, gtf_ref, qb_ref, kb_ref, vb_ref, gb_ref, gtb_ref,
                brow_ref, bcol_ref, hf_ref, hb_ref, cf_scr, nf_scr, mf_scr, cb_scr, nb_scr, mb_scr):
    head = pl.program_id(1)

    @pl.when(pl.program_id(2) == 0)
    def _():
        for ref in (cf_scr, nf_scr, mf_scr, cb_scr, nb_scr, mb_scr):
            ref[...] = jnp.zeros(ref.shape, ref.dtype)

    L = qf_ref.shape[0]
    lane = lax.broadcasted_iota(jnp.int32, (L, LANES), 1)

    def gate_col(g, idx):
        return jnp.sum(jnp.where(lane == head * 4 + idx, g, 0.0), axis=-1, keepdims=True)

    for forward, q_ref, k_ref, v_ref, g_ref, gt_ref, out_ref, c_scr, n_scr, m_scr in (
            (True, qf_ref, kf_ref, vf_ref, gf_ref, gtf_ref, hf_ref, cf_scr, nf_scr, mf_scr),
            (False, qb_ref, kb_ref, vb_ref, gb_ref, gtb_ref, hb_ref, cb_scr, nb_scr, mb_scr)):
        gi = 0 if forward else 2
        g = g_ref[...] + brow_ref[...]
        gt = gt_ref[...] + bcol_ref[...]
        i_row = gt[gi:gi + 1, :]
        f_row = _log_sigmoid(gt[gi + 1:gi + 2, :])
        i_col = gate_col(g, gi)
        f_col = _log_sigmoid(gate_col(g, gi + 1))
        out_ref[...] = _mlstm_chunk(q_ref[...], k_ref[...], v_ref[...], i_row, f_row, i_col, f_col,
                                    c_scr, n_scr, m_scr, forward)


def _mlstm(qk, p, gates, gates_t, bias_row, bias_col, *, batch, seq, heads, head_dim, v_col):
    n = qk.shape[0]
    L = ML_CHUNK
    nc = seq // L
    fw = lambda b, h, c: b * nc + c
    bw = lambda b, h, c: b * nc + (nc - 1 - c)

    def specs(rowf):
        return [
            pl.BlockSpec((L, head_dim), lambda b, h, c: (rowf(b, h, c), h)),
            pl.BlockSpec((L, head_dim), lambda b, h, c: (rowf(b, h, c), heads + h)),
            pl.BlockSpec((L, head_dim), lambda b, h, c: (rowf(b, h, c), v_col + h)),
            pl.BlockSpec((L, LANES), lambda b, h, c: (rowf(b, h, c), 0)),
            pl.BlockSpec((None, 4, L), lambda b, h, c: (h, 0, rowf(b, h, c))),
        ]

    def call_args():
        return [qk, qk, p, gates, gates_t]

    out_spec_f = pl.BlockSpec((L, head_dim), lambda b, h, c: (fw(b, h, c), h))
    out_spec_b = pl.BlockSpec((L, head_dim), lambda b, h, c: (bw(b, h, c), h))
    return pl.pallas_call(
        _mlstm_body,
        grid=(batch, heads, nc),
        in_specs=specs(fw) + specs(bw) + [
            pl.BlockSpec((1, LANES), lambda b, h, c: (0, 0)),
            pl.BlockSpec((None, 4, 1), lambda b, h, c: (h, 0, 0)),
        ],
        out_specs=[out_spec_f, out_spec_b],
        out_shape=[jax.ShapeDtypeStruct((n, heads * head_dim), F32)] * 2,
        scratch_shapes=[
            pltpu.VMEM((head_dim, head_dim), F32), pltpu.VMEM((1, head_dim), F32),
            pltpu.VMEM((1, 1), F32),
            pltpu.VMEM((head_dim, head_dim), F32), pltpu.VMEM((1, head_dim), F32),
            pltpu.VMEM((1, 1), F32),
        ],
        compiler_params=_params(3),
        name="mlstm",
    )(*call_args(), *call_args(), bias_row, bias_col)


def _attn_body(lam_ref, g_ref, q_ref, k_ref, v_ref, bz_ref, o_ref, m_scr, l_scr, acc_scr,
               *, tk, lambda_init):
    tq = q_ref.shape[0]
    seq = k_ref.shape[0]
    q = q_ref[...]
    lane = lax.broadcasted_iota(jnp.int32, (tq, LANES), 1)
    zero = jnp.zeros_like(q)
    q_maps = (jnp.where(lane < LANES // 2, q, zero), jnp.where(lane >= LANES // 2, q, zero))

    m_scr[...] = jnp.full(m_scr.shape, -jnp.inf, F32)
    l_scr[...] = jnp.zeros(l_scr.shape, F32)
    acc_scr[...] = jnp.zeros(acc_scr.shape, F32)

    def step(j, carry):
        start = pl.multiple_of(j * tk, tk)
        k = k_ref[pl.ds(start, tk), :]
        v = v_ref[pl.ds(start, tk), :]
        for mi in range(2):
            s = lax.dot_general(q_maps[mi], k, _NT, preferred_element_type=F32)
            m_prev = m_scr[mi]
            m_cur = jnp.maximum(m_prev, jnp.max(s, axis=-1, keepdims=True))
            alpha = jnp.exp(m_prev - m_cur)
            p = jnp.exp(s - m_cur)
            l_scr[mi] = alpha * l_scr[mi] + jnp.sum(p, axis=-1, keepdims=True)
            acc_scr[mi] = alpha * acc_scr[mi] + jnp.dot(p.astype(BF16), v,
                                                        preferred_element_type=F32)
            m_scr[mi] = m_cur
        return carry

    lax.fori_loop(0, seq // tk, step, 0)

    lp = lam_ref[...]
    dots = jnp.sum(lp[0:1, :] * lp[1:2, :], axis=-1, keepdims=True), \
        jnp.sum(lp[2:3, :] * lp[3:4, :], axis=-1, keepdims=True)
    lam = jnp.exp(dots[0]) - jnp.exp(dots[1]) + lambda_init
    att = acc_scr[0] / l_scr[0] - lam * (acc_scr[1] / l_scr[1])
    ms = jnp.mean(att * att, axis=-1, keepdims=True)
    att = att * lax.rsqrt(ms + NORM_EPS) * g_ref[...] * (1.0 - lambda_init)
    o_ref[...] = (att * _silu(bz_ref[...].astype(F32))).astype(BF16)


def _diff_attn(lam_p, subln_g, rq, rk, p, *, batch, seq, heads, v_col, z_col, tq, tk, lambda_init):
    n = rq.shape[0]
    nq = seq // tq
    body = functools.partial(_attn_body, tk=tk, lambda_init=lambda_init)
    return pl.pallas_call(
        body,
        grid=(batch, heads, nq),
        in_specs=[
            pl.BlockSpec(lam_p.shape, lambda b, h, i: (0, 0)),
            pl.BlockSpec((1, LANES), lambda b, h, i: (0, 0)),
            pl.BlockSpec((tq, LANES), lambda b, h, i: (b * nq + i, h)),
            pl.BlockSpec((seq, LANES), lambda b, h, i: (b, h)),
            pl.BlockSpec((seq, LANES), lambda b, h, i: (b, v_col + h)),
            pl.BlockSpec((tq, LANES), lambda b, h, i: (b * nq + i, z_col + h)),
        ],
        out_specs=pl.BlockSpec((tq, LANES), lambda b, h, i: (b * nq + i, h)),
        out_shape=jax.ShapeDtypeStruct((n, heads * LANES), BF16),
        scratch_shapes=[
            pltpu.VMEM((2, tq, 1), F32), pltpu.VMEM((2, tq, 1), F32),
            pltpu.VMEM((2, tq, LANES), F32),
        ],
        compiler_params=_params(3),
        name="diff_attn",
    )(lam_p, subln_g, rq, rk, p, p)


def _merge_body(hf_ref, hb_ref, ao_ref, az_ref, yb_ref, ga_ref, gb_ref, x_ref, mlg_ref, gbias_ref,
                fing_ref, wa_ref, wb_ref, wo_ref, o_ref, *, head_dim, final_norm):
    hs = hf_ref[...] + hb_ref[...]
    width = hs.shape[1]
    parts = []
    for c0 in range(0, width, head_dim):
        hh = hs[:, c0:c0 + head_dim]
        ms = jnp.mean(hh * hh, axis=-1, keepdims=True)
        parts.append(hh * lax.rsqrt(ms + NORM_EPS))
    hm = jnp.concatenate(parts, axis=-1) * mlg_ref[...]
    y_a = hm * _sigmoid(ao_ref[...].astype(F32)) * _silu(az_ref[...].astype(F32))
    pa = jnp.dot(y_a.astype(BF16), wa_ref[...], preferred_element_type=F32)
    pb = jnp.dot(yb_ref[...], wb_ref[...], preferred_element_type=F32)
    gbias = gbias_ref[...]
    mix = (_sigmoid(ga_ref[...].astype(F32) + gbias[0:1, :]) * pa
           + _sigmoid(gb_ref[...].astype(F32) + gbias[1:2, :]) * pb)
    xn = x_ref[...] + jnp.dot(mix.astype(BF16), wo_ref[...], preferred_element_type=F32)
    if final_norm:
        ms = jnp.mean(xn * xn, axis=-1, keepdims=True)
        xn = xn * lax.rsqrt(ms + NORM_EPS) * fing_ref[...]
    o_ref[...] = xn


def _merge(h_fw, h_bw, p, y_b, x2, ml_g, gate_bias, final_g, w_a, w_b, w_o, *, tm, head_dim,
           o_col, z_col, ga_col, gb_col, final_norm):
    n, d = x2.shape
    row = lambda c: pl.BlockSpec((tm, d), lambda i: (i, c))
    const = lambda shape: pl.BlockSpec(shape, lambda i: (0, 0))
    body = functools.partial(_merge_body, head_dim=head_dim, final_norm=final_norm)
    return pl.pallas_call(
        body,
        grid=(n // tm,),
        in_specs=[row(0), row(0), row(o_col), row(z_col), row(0), row(ga_col), row(gb_col), row(0),
                  const((1, d)), const((2, d)), const((1, d)),
                  const((d, d)), const((d, d)), const((d, d))],
        out_specs=row(0),
        out_shape=jax.ShapeDtypeStruct((n, d), F32),
        compiler_params=_params(1),
        name="merge",
    )(h_fw, h_bw, p, p, y_b, p, p, x2, ml_g, gate_bias, final_g, w_a, w_b, w_o)


def kernel(x, positions, norm_g, w_in, ml_gate_b, ml_conv_w, ml_norm_g, da_lambda, da_subln_g,
           gate_b, w_branch_a, w_branch_b, w_out, final_g):
    batch, seq, d = x.shape
    depth = w_in.shape[0]
    ml_heads, ml_head_dim = ml_norm_g.shape[1], ml_norm_g.shape[2]
    ml_width = ml_heads * ml_head_dim
    n_gates = 4 * ml_heads
    da_v_dim = da_subln_g.shape[1]
    da_qk_dim = da_lambda.shape[2]
    da_width = w_branch_b.shape[1]
    da_heads = da_width // da_v_dim
    assert ml_width == d and da_width == d and da_v_dim == LANES and 2 * da_qk_dim == LANES
    assert w_in.shape[2] == 11 * d + n_gates and seq % ML_CHUNK == 0
    n = batch * seq

    col = {name: i for i, name in enumerate(
        ("a_q", "a_k", "a_v", "a_o", "a_z", "b_q", "b_k", "b_v", "b_z", "g_a", "g_b"))}

    tm_proj = min(1024, seq)
    tm_prep = min(512, seq)
    tq = min(512, seq)
    tk = min(512, seq)
    tm_merge = min(256, seq)

    x2 = x.reshape(n, d)
    pos2 = positions.reshape(n, 1)
    inv_freq = ROPE_THETA ** (-jnp.arange(0, da_qk_dim, 2, dtype=F32) / da_qk_dim)
    freq_tile = jnp.tile(inv_freq, LANES // inv_freq.shape[0]).reshape(1, LANES)

    for layer in range(depth):
        w = w_in[layer]
        w_main = jnp.concatenate([w[:, :5 * d], w[:, 5 * d + n_gates:]], axis=1).astype(BF16)
        w_g = w[:, 5 * d:5 * d + n_gates].reshape(d, 4, ml_heads).transpose(0, 2, 1)
        w_g = w_g.reshape(d, n_gates).astype(BF16)
        wg = jnp.pad(w_g, ((0, 0), (0, LANES - n_gates)))
        bias_hg = ml_gate_b[layer].astype(F32).T
        bias_row = jnp.pad(bias_hg.reshape(1, n_gates), ((0, 0), (0, LANES - n_gates)))
        bias_col = bias_hg.reshape(ml_heads, 4, 1)

        p, gates, gates_t = _in_proj(x2, norm_g[layer].reshape(1, d), w_main, wg, w_g.T,
                                     tm_proj, d)
        qk, rq, rk = _prep(p, pos2, freq_tile, ml_conv_w[layer].astype(F32), seq=seq,
                           ml_width=ml_width, da_width=da_width, bq_col=col["b_q"], tm=tm_prep,
                           k_scale=ml_head_dim ** -0.5, q_scale=da_qk_dim ** -0.5)
        h_fw, h_bw = _mlstm(qk, p, gates, gates_t.reshape(ml_heads, 4, n), bias_row, bias_col,
                            batch=batch, seq=seq, heads=ml_heads, head_dim=ml_head_dim,
                            v_col=col["a_v"] * ml_heads)
        lambda_init = 0.8 - 0.6 * math.exp(-0.3 * layer)
        y_b = _diff_attn(da_lambda[layer].astype(F32), da_subln_g[layer].reshape(1, LANES), rq, rk,
                         p, batch=batch, seq=seq, heads=da_heads, v_col=col["b_v"] * da_heads,
                         z_col=col["b_z"] * da_heads, tq=tq, tk=tk, lambda_init=lambda_init)
        x2 = _merge(h_fw, h_bw, p, y_b, x2, ml_norm_g[layer].reshape(1, ml_width), gate_b[layer],
                    final_g.reshape(1, d), w_branch_a[layer].astype(BF16),
                    w_branch_b[layer].astype(BF16), w_out[layer].astype(BF16), tm=tm_merge,
                    head_dim=ml_head_dim, o_col=col["a_o"], z_col=col["a_z"], ga_col=col["g_a"],
                    gb_col=col["g_b"], final_norm=(layer == depth - 1))
    return x2.reshape(batch, seq, d)
```

```python
import functools
import math

import jax
import jax.numpy as jnp
from jax import lax
from jax.experimental import pallas as pl
from jax.experimental.pallas import tpu as pltpu

F32 = jnp.float32
BF16 = jnp.bfloat16

NORM_EPS = 1e-6
ROPE_THETA = 10000.0
ML_CHUNK = 128
LANES = 128
SUBLANES = 8
VMEM_LIMIT_BYTES = 56 * 1024 * 1024

_NT = (((1,), (1,)), ((), ()))
_TN = (((0,), (0,)), ((), ()))


def _params(n_axes):
    return pltpu.CompilerParams(
        dimension_semantics=("arbitrary",) * n_axes,
        vmem_limit_bytes=VMEM_LIMIT_BYTES)


def _sigmoid(x):
    return 1.0 / (1.0 + jnp.exp(-x))


def _silu(x):
    return x * _sigmoid(x)


def _log_sigmoid(x):
    return jnp.minimum(x, 0.0) - jnp.log1p(jnp.exp(-jnp.abs(x)))


def _in_proj_body(x_ref, g_ref, w_ref, wg_ref, wgt_ref, wvt_ref, p_ref, gates_ref, gatest_ref,
                  vt_ref, h_scr):
    @pl.when(pl.program_id(1) == 0)
    def _():
        x = x_ref[...]
        ms = jnp.mean(x * x, axis=-1, keepdims=True)
        h = (x * lax.rsqrt(ms + NORM_EPS) * g_ref[...]).astype(BF16)
        h_scr[...] = h
        gates_ref[...] = jnp.dot(h, wg_ref[...], preferred_element_type=F32)
        gatest_ref[...] = lax.dot_general(wgt_ref[...], h, _NT, preferred_element_type=F32)
        vt = lax.dot_general(wvt_ref[...], h, _NT, preferred_element_type=F32).astype(BF16)
        tk = vt_ref.shape[2]
        for c in range(vt_ref.shape[0]):
            vt_ref[c] = vt[:, c * tk:(c + 1) * tk]

    p_ref[...] = jnp.dot(h_scr[...], w_ref[...], preferred_element_type=F32).astype(BF16)


def _in_proj(x2, norm_g, w_main, wg, wgt, wvt, tm, tn, tk):
    n, d = x2.shape
    width = w_main.shape[1]
    ng = wgt.shape[0]
    dv = wvt.shape[0]
    return pl.pallas_call(
        _in_proj_body,
        grid=(n // tm, width // tn),
        in_specs=[
            pl.BlockSpec((tm, d), lambda i, j: (i, 0)),
            pl.BlockSpec((1, d), lambda i, j: (0, 0)),
            pl.BlockSpec((d, tn), lambda i, j: (0, j)),
            pl.BlockSpec((d, LANES), lambda i, j: (0, 0)),
            pl.BlockSpec((ng, d), lambda i, j: (0, 0)),
            pl.BlockSpec((dv, d), lambda i, j: (0, 0)),
        ],
        out_specs=[
            pl.BlockSpec((tm, tn), lambda i, j: (i, j)),
            pl.BlockSpec((tm, LANES), lambda i, j: (i, 0)),
            pl.BlockSpec((ng, tm), lambda i, j: (0, i)),
            pl.BlockSpec((tm // tk, dv, tk), lambda i, j: (i, 0, 0)),
        ],
        out_shape=[
            jax.ShapeDtypeStruct((n, width), BF16),
            jax.ShapeDtypeStruct((n, LANES), F32),
            jax.ShapeDtypeStruct((ng, n), F32),
            jax.ShapeDtypeStruct((n // tk, dv, tk), BF16),
        ],
        scratch_shapes=[pltpu.VMEM((tm, d), BF16)],
        compiler_params=_params(2),
        name="in_proj",
    )(x2, norm_g, w_main, wg, wgt, wvt)


def _prep_body(qk_ref, prev_ref, next_ref, bq_ref, bk_ref, pos_ref, freq_ref, cw_ref,
               qk_out, rq_out, rk_out, ext_scr, *, tiles_per_seq, conv_width, k_scale, q_scale,
               n_heads):
    i = pl.program_id(0)
    tm = qk_ref.shape[0]
    width = qk_ref.shape[1]
    pad = conv_width // 2
    halo = SUBLANES
    t_in_seq = i % tiles_per_seq
    has_prev = (t_in_seq > 0).astype(F32)
    has_next = (t_in_seq < tiles_per_seq - 1).astype(F32)

    slab = ext_scr.shape[1]
    for c0 in range(0, width, slab):
        cols = slice(c0, c0 + slab)
        ext_scr[0:halo, :] = prev_ref[:, cols].astype(F32) * has_prev
        ext_scr[halo:halo + tm, :] = qk_ref[:, cols].astype(F32)
        ext_scr[halo + tm:halo + tm + halo, :] = next_ref[:, cols].astype(F32) * has_next
        acc = None
        for t in range(conv_width):
            term = ext_scr[pl.ds(halo - pad + t, tm), :] * cw_ref[t:t + 1, cols]
            acc = term if acc is None else acc + term
        y = _silu(acc)
        scale = 1.0 if c0 < width // 2 else k_scale
        qk_out[:, cols] = (y * scale).astype(BF16)

    ang = pos_ref[...].astype(F32) * freq_ref[...]
    cos = jnp.cos(ang)
    sin = jnp.sin(ang)
    lane = lax.broadcasted_iota(jnp.int32, (tm, LANES), 1)
    d_map = LANES // 2
    low = (lane % d_map) < (d_map // 2)
    sin_signed = jnp.where(low, -sin, sin)
    for src, dst, scale in ((bq_ref, rq_out, q_scale), (bk_ref, rk_out, 1.0)):
        for h in range(n_heads):
            cols = slice(h * LANES, (h + 1) * LANES)
            t = src[:, cols].astype(F32)
            up = pltpu.roll(t, LANES - d_map // 2, 1)
            down = pltpu.roll(t, d_map // 2, 1)
            rot = jnp.where(low, up, down)
            dst[:, cols] = ((t * cos + rot * sin_signed) * scale).astype(BF16)


def _prep(p, pos2, freq_tile, conv_w, *, seq, ml_width, da_width, bq_col, tm, k_scale, q_scale):
    n = p.shape[0]
    n_tiles = n // tm
    halo_blocks = n // SUBLANES
    qk_w = 2 * ml_width
    body = functools.partial(
        _prep_body, tiles_per_seq=seq // tm, conv_width=conv_w.shape[0], k_scale=k_scale,
        q_scale=q_scale, n_heads=da_width // LANES)
    per_tile = tm // SUBLANES
    return pl.pallas_call(
        body,
        grid=(n_tiles,),
        in_specs=[
            pl.BlockSpec((tm, qk_w), lambda i: (i, 0)),
            pl.BlockSpec((SUBLANES, qk_w), lambda i: (jnp.maximum(i * per_tile - 1, 0), 0)),
            pl.BlockSpec((SUBLANES, qk_w),
                         lambda i: (jnp.minimum((i + 1) * per_tile, halo_blocks - 1), 0)),
            pl.BlockSpec((tm, da_width), lambda i: (i, bq_col)),
            pl.BlockSpec((tm, da_width), lambda i: (i, bq_col + 1)),
            pl.BlockSpec((tm, 1), lambda i: (i, 0)),
            pl.BlockSpec((1, LANES), lambda i: (0, 0)),
            pl.BlockSpec((conv_w.shape[0], qk_w), lambda i: (0, 0)),
        ],
        out_specs=[
            pl.BlockSpec((tm, qk_w), lambda i: (i, 0)),
            pl.BlockSpec((tm, da_width), lambda i: (i, 0)),
            pl.BlockSpec((tm, da_width), lambda i: (i, 0)),
        ],
        out_shape=[
            jax.ShapeDtypeStruct((n, qk_w), BF16),
            jax.ShapeDtypeStruct((n, da_width), BF16),
            jax.ShapeDtypeStruct((n, da_width), BF16),
        ],
        scratch_shapes=[pltpu.VMEM((tm + 2 * SUBLANES, 256), F32)],
        compiler_params=_params(1),
        name="prep",
    )(p, p, p, p, p, pos2, freq_tile, conv_w)


def _split3(f):
    hi = f.astype(BF16)
    r1 = f - hi.astype(F32)
    mid = r1.astype(BF16)
    lo = (r1 - mid.astype(F32)).astype(BF16)
    return hi, mid, lo


def _mlstm_chunk(qc, kc, vc, i_row, f_row, i_col, f_col, c_ref, n_ref, m_ref, forward):
    L = qc.shape[0]
    r = lax.broadcasted_iota(jnp.int32, (L, L), 0)
    c = lax.broadcasted_iota(jnp.int32, (L, L), 1)
    valid = (c <= r) if forward else (c >= r)
    tri_left = valid.astype(BF16)
    tri_right = ((r <= c) if forward else (r >= c)).astype(BF16)

    f_b = jnp.broadcast_to(f_col, (L, LANES))
    b_col = sum(jnp.dot(tri_left, part, preferred_element_type=F32) for part in _split3(f_b))
    f_r = jnp.broadcast_to(f_row, (SUBLANES, L))
    b_row = sum(jnp.dot(part, tri_right, preferred_element_type=F32) for part in _split3(f_r))
    b_c = b_col[:, 0:1]
    a_row = i_row - b_row[0:1, :]
    a_col = i_col - b_c

    m_st = m_ref[...]
    log_d = jnp.where(valid, b_col + a_row, -jnp.inf)
    m_t = jnp.maximum(b_c + m_st, jnp.max(log_d, axis=-1, keepdims=True))
    s_qk = lax.dot_general(qc, kc, _NT, preferred_element_type=F32)
    w_intra = s_qk * jnp.exp(log_d - m_t)
    s_inter = jnp.exp(b_c + m_st - m_t)
    c_mat = c_ref[...]
    inter = lax.dot_general(qc, c_mat.astype(BF16), _NT, preferred_element_type=F32)
    num = jnp.dot(w_intra.astype(BF16), vc, preferred_element_type=F32) + s_inter * inter
    n_vec = n_ref[...]
    qn = jnp.sum(qc.astype(F32) * n_vec, axis=-1, keepdims=True)
    den = jnp.sum(w_intra, axis=-1, keepdims=True) + s_inter * qn
    h = num / jnp.maximum(jnp.abs(den), jnp.exp(-m_t))

    b_last = b_c[L - 1:L, :] if forward else b_c[0:1, :]
    m_new = jnp.maximum(b_last + m_st, jnp.max(b_last + a_row, axis=-1, keepdims=True))
    w_col = jnp.exp(b_last + a_col - m_new)
    decay = jnp.exp(b_last + m_st - m_new)
    wv = (w_col * vc.astype(F32)).astype(BF16)
    c_ref[...] = decay * c_mat + lax.dot_general(wv, kc, _TN, preferred_element_type=F32)
    n_ref[...] = decay * n_vec + jnp.sum(w_col * kc.astype(F32), axis=0, keepdims=True)
    m_ref[...] = m_new
    return h


def _mlstm_body(qf_ref, kf_ref, vf_ref, gf_ref, gtf_ref, qb_ref, kb_ref, vb_ref, gb_ref, gtb_ref,
                brow_ref, bcol_ref, hf_ref, hb_ref, cf_scr, nf_scr, mf_scr, cb_scr, nb_scr, mb_scr):
    head = pl.program_id(1)

    @pl.when(pl.program_id(2) == 0)
    def _():
        for ref in (cf_scr, nf_scr, mf_scr, cb_scr, nb_scr, mb_scr):
            ref[...] = jnp.zeros(ref.shape, ref.dtype)

    L = qf_ref.shape[0]
    lane = lax.broadcasted_iota(jnp.int32, (L, LANES), 1)

    def gate_col(g, idx):
        return jnp.sum(jnp.where(lane == head * 4 + idx, g, 0.0), axis=-1, keepdims=True)

    for forward, q_ref, k_ref, v_ref, g_ref, gt_ref, out_ref, c_scr, n_scr, m_scr in (
            (True, qf_ref, kf_ref, vf_ref, gf_ref, gtf_ref, hf_ref, cf_scr, nf_scr, mf_scr),
            (False, qb_ref, kb_ref, vb_ref, gb_ref, gtb_ref, hb_ref, cb_scr, nb_scr, mb_scr)):
        gi = 0 if forward else 2
        g = g_ref[...] + brow_ref[...]
        gt = gt_ref[...] + bcol_ref[...]
        i_row = gt[gi:gi + 1, :]
        f_row = _log_sigmoid(gt[gi + 1:gi + 2, :])
        i_col = gate_col(g, gi)
        f_col = _log_sigmoid(gate_col(g, gi + 1))
        out_ref[...] = _mlstm_chunk(q_ref[...], k_ref[...], v_ref[...], i_row, f_row, i_col, f_col,
                                    c_scr, n_scr, m_scr, forward)


def _mlstm(qk, p, gates, gates_t, bias_row, bias_col, *, batch, seq, heads, head_dim, v_col):
    n = qk.shape[0]
    L = ML_CHUNK
    nc = seq // L
    fw = lambda b, h, c: b * nc + c
    bw = lambda b, h, c: b * nc + (nc - 1 - c)

    def specs(rowf):
        return [
            pl.BlockSpec((L, head_dim), lambda b, h, c: (rowf(b, h, c), h)),
            pl.BlockSpec((L, head_dim), lambda b, h, c: (rowf(b, h, c), heads + h)),
            pl.BlockSpec((L, head_dim), lambda b, h, c: (rowf(b, h, c), v_col + h)),
            pl.BlockSpec((L, LANES), lambda b, h, c: (rowf(b, h, c), 0)),
            pl.BlockSpec((None, 4, L), lambda b, h, c: (h, 0, rowf(b, h, c))),
        ]

    def call_args():
        return [qk, qk, p, gates, gates_t]

    out_spec_f = pl.BlockSpec((L, head_dim), lambda b, h, c: (fw(b, h, c), h))
    out_spec_b = pl.BlockSpec((L, head_dim), lambda b, h, c: (bw(b, h, c), h))
    return pl.pallas_call(
        _mlstm_body,
        grid=(batch, heads, nc),
        in_specs=specs(fw) + specs(bw) + [
            pl.BlockSpec((1, LANES), lambda b, h, c: (0, 0)),
            pl.BlockSpec((None, 4, 1), lambda b, h, c: (h, 0, 0)),
        ],
        out_specs=[out_spec_f, out_spec_b],
        out_shape=[jax.ShapeDtypeStruct((n, heads * head_dim), F32)] * 2,
        scratch_shapes=[
            pltpu.VMEM((head_dim, head_dim), F32), pltpu.VMEM((1, head_dim), F32),
            pltpu.VMEM((1, 1), F32),
            pltpu.VMEM((head_dim, head_dim), F32), pltpu.VMEM((1, head_dim), F32),
            pltpu.VMEM((1, 1), F32),
        ],
        compiler_params=_params(3),
        name="mlstm",
    )(*call_args(), *call_args(), bias_row, bias_col)


def _attn_body(lam_ref, g_ref, q_ref, k_ref, vt_ref, bz_ref, o_ref, m_scr, l_scr, acc_scr,
               s_scr0, s_scr1, p_scr0, p_scr1, a_scr0, a_scr1, x_scr0, x_scr1, *, lambda_init):
    tq = q_ref.shape[0]
    n_chunks, _, tk = vt_ref.shape
    q = q_ref[...]
    lane = lax.broadcasted_iota(jnp.int32, (tq, LANES), 1)
    zero = jnp.zeros_like(q)
    q_maps = (jnp.where(lane < LANES // 2, q, zero), jnp.where(lane >= LANES // 2, q, zero))
    s_scr, p_scr, a_scr = (s_scr0, s_scr1), (p_scr0, p_scr1), (a_scr0, a_scr1)
    x_scr = (x_scr0, x_scr1)

    m_scr[...] = jnp.full(m_scr.shape, -jnp.inf, F32)
    l_scr[...] = jnp.zeros(l_scr.shape, F32)
    acc_scr[...] = jnp.zeros(acc_scr.shape, F32)

    def scores(j, slot):
        k = k_ref[pl.ds(pl.multiple_of(j * tk, tk), tk), :]
        for mi in range(2):
            s = lax.dot_general(k, q_maps[mi], _NT, preferred_element_type=F32)
            s_scr[slot][mi] = s
            x_scr[slot][mi] = jnp.max(s, axis=0, keepdims=True)

    def softmax(slot):
        for mi in range(2):
            m_prev = m_scr[mi]
            m_cur = jnp.maximum(m_prev, x_scr[slot][mi])
            alpha = jnp.exp2(m_prev - m_cur)
            p = jnp.exp2(s_scr[slot][mi] - m_cur)
            l_scr[mi] = alpha * l_scr[mi] + jnp.sum(p, axis=0, keepdims=True)
            m_scr[mi] = m_cur
            a_scr[slot][mi] = alpha
            p_scr[slot][mi] = p.astype(BF16)

    def values(j, slot):
        vt = vt_ref[j]
        for mi in range(2):
            acc_scr[mi] = a_scr[slot][mi] * acc_scr[mi] + jnp.dot(
                vt, p_scr[slot][mi], preferred_element_type=F32)

    scores(0, 0)
    scores(1, 1)
    softmax(0)

    def two_steps(t, carry):
        j = 2 * t + 1
        scores(j + 1, 0)
        softmax(1)
        values(j - 1, 0)
        scores(j + 2, 1)
        softmax(0)
        values(j, 1)
        return carry

    lax.fori_loop(0, (n_chunks - 2) // 2, two_steps, 0)
    softmax(1)
    values(n_chunks - 2, 0)
    values(n_chunks - 1, 1)

    lp = lam_ref[...]
    dots = jnp.sum(lp[0:1, :] * lp[1:2, :], axis=-1, keepdims=True), \
        jnp.sum(lp[2:3, :] * lp[3:4, :], axis=-1, keepdims=True)
    lam = jnp.exp(dots[0]) - jnp.exp(dots[1]) + lambda_init
    att = acc_scr[0] / l_scr[0] - lam * (acc_scr[1] / l_scr[1])
    ms = jnp.mean(att * att, axis=0, keepdims=True)
    att = (att * lax.rsqrt(ms + NORM_EPS)).T
    att = att * g_ref[...] * (1.0 - lambda_init)
    o_ref[...] = (att * _silu(bz_ref[...].astype(F32))).astype(BF16)


def _diff_attn(lam_p, subln_g, rq, rk, vt, p, *, batch, seq, heads, z_col, tq, lambda_init):
    n = rq.shape[0]
    nq = seq // tq
    tk = vt.shape[2]
    nk = seq // tk
    body = functools.partial(_attn_body, lambda_init=lambda_init)
    return pl.pallas_call(
        body,
        grid=(batch, heads, nq),
        in_specs=[
            pl.BlockSpec(lam_p.shape, lambda b, h, i: (0, 0)),
            pl.BlockSpec((1, LANES), lambda b, h, i: (0, 0)),
            pl.BlockSpec((tq, LANES), lambda b, h, i: (b * nq + i, h)),
            pl.BlockSpec((seq, LANES), lambda b, h, i: (b, h)),
            pl.BlockSpec((nk, LANES, tk), lambda b, h, i: (b, h, 0)),
            pl.BlockSpec((tq, LANES), lambda b, h, i: (b * nq + i, z_col + h)),
        ],
        out_specs=pl.BlockSpec((tq, LANES), lambda b, h, i: (b * nq + i, h)),
        out_shape=jax.ShapeDtypeStruct((n, heads * LANES), BF16),
        scratch_shapes=[
            pltpu.VMEM((2, 1, tq), F32), pltpu.VMEM((2, 1, tq), F32),
            pltpu.VMEM((2, LANES, tq), F32),
            pltpu.VMEM((2, tk, tq), F32), pltpu.VMEM((2, tk, tq), F32),
            pltpu.VMEM((2, tk, tq), BF16), pltpu.VMEM((2, tk, tq), BF16),
            pltpu.VMEM((2, 1, tq), F32), pltpu.VMEM((2, 1, tq), F32),
            pltpu.VMEM((2, 1, tq), F32), pltpu.VMEM((2, 1, tq), F32),
        ],
        compiler_params=_params(3),
        name="diff_attn",
    )(lam_p, subln_g, rq, rk, vt, p)


def _merge_body(hf_ref, hb_ref, ao_ref, az_ref, yb_ref, ga_ref, gb_ref, x_ref, mlg_ref, gbias_ref,
                fing_ref, wa_ref, wb_ref, wo_ref, o_ref, *, head_dim, final_norm):
    hs = hf_ref[...] + hb_ref[...]
    width = hs.shape[1]
    parts = []
    for c0 in range(0, width, head_dim):
        hh = hs[:, c0:c0 + head_dim]
        ms = jnp.mean(hh * hh, axis=-1, keepdims=True)
        parts.append(hh * lax.rsqrt(ms + NORM_EPS))
    hm = jnp.concatenate(parts, axis=-1) * mlg_ref[...]
    y_a = hm * _sigmoid(ao_ref[...].astype(F32)) * _silu(az_ref[...].astype(F32))
    pa = jnp.dot(y_a.astype(BF16), wa_ref[...], preferred_element_type=F32)
    pb = jnp.dot(yb_ref[...], wb_ref[...], preferred_element_type=F32)
    gbias = gbias_ref[...]
    mix = (_sigmoid(ga_ref[...].astype(F32) + gbias[0:1, :]) * pa
           + _sigmoid(gb_ref[...].astype(F32) + gbias[1:2, :]) * pb)
    xn = x_ref[...] + jnp.dot(mix.astype(BF16), wo_ref[...], preferred_element_type=F32)
    if final_norm:
        ms = jnp.mean(xn * xn, axis=-1, keepdims=True)
        xn = xn * lax.rsqrt(ms + NORM_EPS) * fing_ref[...]
    o_ref[...] = xn


def _merge(h_fw, h_bw, p, y_b, x2, ml_g, gate_bias, final_g, w_a, w_b, w_o, *, tm, head_dim,
           o_col, z_col, ga_col, gb_col, final_norm):
    n, d = x2.shape
    row = lambda c: pl.BlockSpec((tm, d), lambda i: (i, c))
    const = lambda shape: pl.BlockSpec(shape, lambda i: (0, 0))
    body = functools.partial(_merge_body, head_dim=head_dim, final_norm=final_norm)
    return pl.pallas_call(
        body,
        grid=(n // tm,),
        in_specs=[row(0), row(0), row(o_col), row(z_col), row(0), row(ga_col), row(gb_col), row(0),
                  const((1, d)), const((2, d)), const((1, d)),
                  const((d, d)), const((d, d)), const((d, d))],
        out_specs=row(0),
        out_shape=jax.ShapeDtypeStruct((n, d), F32),
        compiler_params=_params(1),
        name="merge",
    )(h_fw, h_bw, p, p, y_b, p, p, x2, ml_g, gate_bias, final_g, w_a, w_b, w_o)


def kernel(x, positions, norm_g, w_in, ml_gate_b, ml_conv_w, ml_norm_g, da_lambda, da_subln_g,
           gate_b, w_branch_a, w_branch_b, w_out, final_g):
    batch, seq, d = x.shape
    depth = w_in.shape[0]
    ml_heads, ml_head_dim = ml_norm_g.shape[1], ml_norm_g.shape[2]
    ml_width = ml_heads * ml_head_dim
    n_gates = 4 * ml_heads
    da_v_dim = da_subln_g.shape[1]
    da_qk_dim = da_lambda.shape[2]
    da_width = w_branch_b.shape[1]
    da_heads = da_width // da_v_dim
    assert ml_width == d and da_width == d and da_v_dim == LANES and 2 * da_qk_dim == LANES
    assert w_in.shape[2] == 11 * d + n_gates and seq % ML_CHUNK == 0
    n = batch * seq

    col = {name: i for i, name in enumerate(
        ("a_q", "a_k", "a_v", "a_o", "a_z", "b_q", "b_k", "b_z", "g_a", "g_b"))}
    bv0 = 7 * d + n_gates

    tm_proj = min(1024, seq)
    tm_prep = min(512, seq)
    tq = min(512, seq)
    tk = min(512, seq)
    tm_merge = min(256, seq)

    x2 = x.reshape(n, d)
    pos2 = positions.reshape(n, 1)
    inv_freq = ROPE_THETA ** (-jnp.arange(0, da_qk_dim, 2, dtype=F32) / da_qk_dim)
    freq_tile = jnp.tile(inv_freq, LANES // inv_freq.shape[0]).reshape(1, LANES)

    for layer in range(depth):
        w = w_in[layer]
        w_main = jnp.concatenate([w[:, :5 * d], w[:, 5 * d + n_gates:bv0], w[:, bv0 + d:]],
                                 axis=1).astype(BF16)
        wvt = w[:, bv0:bv0 + d].T.astype(BF16)
        w_g = w[:, 5 * d:5 * d + n_gates].reshape(d, 4, ml_heads).transpose(0, 2, 1)
        w_g = w_g.reshape(d, n_gates).astype(BF16)
        wg = jnp.pad(w_g, ((0, 0), (0, LANES - n_gates)))
        bias_hg = ml_gate_b[layer].astype(F32).T
        bias_row = jnp.pad(bias_hg.reshape(1, n_gates), ((0, 0), (0, LANES - n_gates)))
        bias_col = bias_hg.reshape(ml_heads, 4, 1)

        p, gates, gates_t, vt = _in_proj(x2, norm_g[layer].reshape(1, d), w_main, wg, w_g.T, wvt,
                                         tm_proj, d, tk)
        qk, rq, rk = _prep(p, pos2, freq_tile, ml_conv_w[layer].astype(F32), seq=seq,
                           ml_width=ml_width, da_width=da_width, bq_col=col["b_q"], tm=tm_prep,
                           k_scale=ml_head_dim ** -0.5,
                           q_scale=da_qk_dim ** -0.5 * math.log2(math.e))
        h_fw, h_bw = _mlstm(qk, p, gates, gates_t.reshape(ml_heads, 4, n), bias_row, bias_col,
                            batch=batch, seq=seq, heads=ml_heads, head_dim=ml_head_dim,
                            v_col=col["a_v"] * ml_heads)
        lambda_init = 0.8 - 0.6 * math.exp(-0.3 * layer)
        y_b = _diff_attn(da_lambda[layer].astype(F32), da_subln_g[layer].reshape(1, LANES), rq, rk,
                         vt, p, batch=batch, seq=seq, heads=da_heads,
                         z_col=col["b_z"] * da_heads, tq=tq, lambda_init=lambda_init)
        x2 = _merge(h_fw, h_bw, p, y_b, x2, ml_norm_g[layer].reshape(1, ml_width), gate_b[layer],
                    final_g.reshape(1, d), w_branch_a[layer].astype(BF16),
                    w_branch_b[layer].astype(BF16), w_out[layer].astype(BF16), tm=tm_merge,
                    head_dim=ml_head_dim, o_col=col["a_o"], z_col=col["a_z"], ga_col=col["g_a"],
                    gb_col=col["g_b"], final_norm=(layer == depth - 1))
    return x2.reshape(batch, seq, d)
```

```python
import functools
import math

import jax
import jax.numpy as jnp
from jax import lax
from jax.experimental import pallas as pl
from jax.experimental.pallas import tpu as pltpu

F32 = jnp.float32
BF16 = jnp.bfloat16

NORM_EPS = 1e-6
ROPE_THETA = 10000.0
ML_CHUNK = 128
LANES = 128
SUBLANES = 8
ONES_ROWS = 16
VMEM_LIMIT_BYTES = 56 * 1024 * 1024

_NT = (((1,), (1,)), ((), ()))
_TN = (((0,), (0,)), ((), ()))


def _params(n_axes):
    return pltpu.CompilerParams(
        dimension_semantics=("arbitrary",) * n_axes,
        vmem_limit_bytes=VMEM_LIMIT_BYTES)


def _sigmoid(x):
    return 1.0 / (1.0 + jnp.exp(-x))


def _silu(x):
    return x * _sigmoid(x)


def _log_sigmoid(x):
    return jnp.minimum(x, 0.0) - jnp.log1p(jnp.exp(-jnp.abs(x)))


def _in_proj_body(x_ref, g_ref, w_ref, wg_ref, wgt_ref, wvt_ref, wavt_ref, p_ref, gates_ref,
                  gatest_ref, vt_ref, avt_ref, h_scr):
    @pl.when(pl.program_id(1) == 0)
    def _():
        x = x_ref[...]
        ms = jnp.mean(x * x, axis=-1, keepdims=True)
        h = (x * lax.rsqrt(ms + NORM_EPS) * g_ref[...]).astype(BF16)
        h_scr[...] = h
        gates_ref[...] = jnp.dot(h, wg_ref[...], preferred_element_type=F32)
        gatest_ref[...] = lax.dot_general(wgt_ref[...], h, _NT, preferred_element_type=F32)
        avt_ref[...] = lax.dot_general(wavt_ref[...], h, _NT,
                                       preferred_element_type=F32).astype(BF16)
        vt = lax.dot_general(wvt_ref[...], h, _NT, preferred_element_type=F32).astype(BF16)
        tk = vt_ref.shape[2]
        for c in range(vt_ref.shape[0]):
            vt_ref[c] = vt[:, c * tk:(c + 1) * tk]

    p_ref[...] = jnp.dot(h_scr[...], w_ref[...], preferred_element_type=F32).astype(BF16)


def _in_proj(x2, norm_g, w_main, wg, wgt, wvt, wavt, tm, tn, tk):
    n, d = x2.shape
    width = w_main.shape[1]
    ng = wgt.shape[0]
    dv = wvt.shape[0]
    dav = wavt.shape[0]
    return pl.pallas_call(
        _in_proj_body,
        grid=(n // tm, width // tn),
        in_specs=[
            pl.BlockSpec((tm, d), lambda i, j: (i, 0)),
            pl.BlockSpec((1, d), lambda i, j: (0, 0)),
            pl.BlockSpec((d, tn), lambda i, j: (0, j)),
            pl.BlockSpec((d, LANES), lambda i, j: (0, 0)),
            pl.BlockSpec((ng, d), lambda i, j: (0, 0)),
            pl.BlockSpec((dv, d), lambda i, j: (0, 0)),
            pl.BlockSpec((dav, d), lambda i, j: (0, 0)),
        ],
        out_specs=[
            pl.BlockSpec((tm, tn), lambda i, j: (i, j)),
            pl.BlockSpec((tm, LANES), lambda i, j: (i, 0)),
            pl.BlockSpec((ng, tm), lambda i, j: (0, i)),
            pl.BlockSpec((tm // tk, dv, tk), lambda i, j: (i, 0, 0)),
            pl.BlockSpec((dav, tm), lambda i, j: (0, i)),
        ],
        out_shape=[
            jax.ShapeDtypeStruct((n, width), BF16),
            jax.ShapeDtypeStruct((n, LANES), F32),
            jax.ShapeDtypeStruct((ng, n), F32),
            jax.ShapeDtypeStruct((n // tk, dv, tk), BF16),
            jax.ShapeDtypeStruct((dav, n), BF16),
        ],
        scratch_shapes=[pltpu.VMEM((tm, d), BF16)],
        compiler_params=_params(2),
        name="in_proj",
    )(x2, norm_g, w_main, wg, wgt, wvt, wavt)


def _prep_body(qk_ref, prev_ref, next_ref, bq_ref, bk_ref, pos_ref, freq_ref, cw_ref,
               qk_out, rq_out, rk_out, ext_scr, *, tiles_per_seq, conv_width, k_scale, q_scale,
               n_heads):
    i = pl.program_id(0)
    tm = qk_ref.shape[0]
    width = qk_ref.shape[1]
    pad = conv_width // 2
    halo = SUBLANES
    t_in_seq = i % tiles_per_seq
    has_prev = (t_in_seq > 0).astype(F32)
    has_next = (t_in_seq < tiles_per_seq - 1).astype(F32)

    slab = ext_scr.shape[1]
    for c0 in range(0, width, slab):
        cols = slice(c0, c0 + slab)
        ext_scr[0:halo, :] = prev_ref[:, cols].astype(F32) * has_prev
        ext_scr[halo:halo + tm, :] = qk_ref[:, cols].astype(F32)
        ext_scr[halo + tm:halo + tm + halo, :] = next_ref[:, cols].astype(F32) * has_next
        acc = None
        for t in range(conv_width):
            term = ext_scr[pl.ds(halo - pad + t, tm), :] * cw_ref[t:t + 1, cols]
            acc = term if acc is None else acc + term
        y = _silu(acc)
        scale = 1.0 if c0 < width // 2 else k_scale
        qk_out[:, cols] = (y * scale).astype(BF16)

    ang = pos_ref[...].astype(F32) * freq_ref[...]
    cos = jnp.cos(ang)
    sin = jnp.sin(ang)
    lane = lax.broadcasted_iota(jnp.int32, (tm, LANES), 1)
    d_map = LANES // 2
    low = (lane % d_map) < (d_map // 2)
    sin_signed = jnp.where(low, -sin, sin)
    for src, dst, scale in ((bq_ref, rq_out, q_scale), (bk_ref, rk_out, 1.0)):
        for h in range(n_heads):
            cols = slice(h * LANES, (h + 1) * LANES)
            t = src[:, cols].astype(F32)
            up = pltpu.roll(t, LANES - d_map // 2, 1)
            down = pltpu.roll(t, d_map // 2, 1)
            rot = jnp.where(low, up, down)
            dst[:, cols] = ((t * cos + rot * sin_signed) * scale).astype(BF16)


def _prep(p, pos2, freq_tile, conv_w, *, seq, ml_width, da_width, bq_col, tm, k_scale, q_scale):
    n = p.shape[0]
    n_tiles = n // tm
    halo_blocks = n // SUBLANES
    qk_w = 2 * ml_width
    body = functools.partial(
        _prep_body, tiles_per_seq=seq // tm, conv_width=conv_w.shape[0], k_scale=k_scale,
        q_scale=q_scale, n_heads=da_width // LANES)
    per_tile = tm // SUBLANES
    return pl.pallas_call(
        body,
        grid=(n_tiles,),
        in_specs=[
            pl.BlockSpec((tm, qk_w), lambda i: (i, 0)),
            pl.BlockSpec((SUBLANES, qk_w), lambda i: (jnp.maximum(i * per_tile - 1, 0), 0)),
            pl.BlockSpec((SUBLANES, qk_w),
                         lambda i: (jnp.minimum((i + 1) * per_tile, halo_blocks - 1), 0)),
            pl.BlockSpec((tm, da_width), lambda i: (i, bq_col)),
            pl.BlockSpec((tm, da_width), lambda i: (i, bq_col + 1)),
            pl.BlockSpec((tm, 1), lambda i: (i, 0)),
            pl.BlockSpec((1, LANES), lambda i: (0, 0)),
            pl.BlockSpec((conv_w.shape[0], qk_w), lambda i: (0, 0)),
        ],
        out_specs=[
            pl.BlockSpec((tm, qk_w), lambda i: (i, 0)),
            pl.BlockSpec((tm, da_width), lambda i: (i, 0)),
            pl.BlockSpec((tm, da_width), lambda i: (i, 0)),
        ],
        out_shape=[
            jax.ShapeDtypeStruct((n, qk_w), BF16),
            jax.ShapeDtypeStruct((n, da_width), BF16),
            jax.ShapeDtypeStruct((n, da_width), BF16),
        ],
        scratch_shapes=[pltpu.VMEM((tm + 2 * SUBLANES, 256), F32)],
        compiler_params=_params(1),
        name="prep",
    )(p, p, p, p, p, pos2, freq_tile, conv_w)


def _split3(f):
    hi = f.astype(BF16)
    r1 = f - hi.astype(F32)
    mid = r1.astype(BF16)
    lo = (r1 - mid.astype(F32)).astype(BF16)
    return hi, mid, lo


def _mlstm_chunk(qc, kc, vt, i_row, b_row, a_col, valid_t, c_ref, m_ref, forward):
    L = qc.shape[0]
    d = vt.shape[0]
    a_row = i_row - b_row
    m_st = m_ref[...]
    log_d = jnp.where(valid_t, a_col + b_row, -jnp.inf)
    m_t = jnp.maximum(b_row + m_st, jnp.max(log_d, axis=0, keepdims=True))
    w_intra = lax.dot_general(kc, qc, _NT, preferred_element_type=F32) * jnp.exp(log_d - m_t)
    s_inter = jnp.exp(b_row + m_st - m_t)
    c_aug = c_ref[...]
    inter = lax.dot_general(c_aug.astype(BF16), qc, _NT, preferred_element_type=F32)
    num = jnp.dot(vt, w_intra.astype(BF16), preferred_element_type=F32) + s_inter * inter[:d]
    den = jnp.sum(w_intra, axis=0, keepdims=True) + s_inter * inter[d:d + 1]
    h_t = num / jnp.maximum(jnp.abs(den), jnp.exp(-m_t))

    b_last = b_row[:, L - 1:L] if forward else b_row[:, 0:1]
    m_new = jnp.maximum(b_last + m_st, jnp.max(b_last + a_row, axis=-1, keepdims=True))
    w_row = jnp.exp(b_last + a_row - m_new)
    decay = jnp.exp(b_last + m_st - m_new)
    lhs = jnp.concatenate([vt.astype(F32) * w_row, jnp.broadcast_to(w_row, (ONES_ROWS, L))], axis=0)
    c_ref[...] = decay * c_aug + jnp.dot(lhs.astype(BF16), kc, preferred_element_type=F32)
    m_ref[...] = m_new
    return h_t.T


def _mlstm_body(qf_ref, kf_ref, vf_ref, gf_ref, gtf_ref, qb_ref, kb_ref, vb_ref, gb_ref, gtb_ref,
                brow_ref, bcol_ref, hf_ref, hb_ref, c_scr, m_scr, *, heads):
    @pl.when(pl.program_id(1) == 0)
    def _():
        for ref in (c_scr, m_scr):
            ref[...] = jnp.zeros(ref.shape, ref.dtype)

    L = qf_ref.shape[0]
    d = qf_ref.shape[1] // heads
    r = lax.broadcasted_iota(jnp.int32, (L, L), 0)
    c = lax.broadcasted_iota(jnp.int32, (L, L), 1)
    lower, upper = c <= r, c >= r
    tri = {True: lower.astype(BF16), False: upper.astype(BF16)}

    for forward, q_ref, k_ref, v_ref, g_ref, gt_ref, out_ref in (
            (True, qf_ref, kf_ref, vf_ref, gf_ref, gtf_ref, hf_ref),
            (False, qb_ref, kb_ref, vb_ref, gb_ref, gtb_ref, hb_ref)):
        gi = 0 if forward else 2
        g = g_ref[...] + brow_ref[...]
        gt = gt_ref[...] + bcol_ref[...]
        cum_col = sum(jnp.dot(tri[forward], part, preferred_element_type=F32)
                      for part in _split3(_log_sigmoid(g)))
        cum_row = sum(jnp.dot(part, tri[not forward], preferred_element_type=F32)
                      for part in _split3(_log_sigmoid(gt)))
        valid_t = upper if forward else lower
        for h in range(heads):
            ii, fi = h * 4 + gi, h * 4 + gi + 1
            cols = slice(h * d, (h + 1) * d)
            ci = h if forward else heads + h
            out_ref[:, cols] = _mlstm_chunk(
                q_ref[:, cols], k_ref[:, cols], v_ref[cols, :],
                gt[ii:ii + 1, :], cum_row[fi:fi + 1, :], g[:, ii:ii + 1] - cum_col[:, fi:fi + 1],
                valid_t, c_scr.at[ci], m_scr.at[ci], forward)


def _mlstm(qk, avt, gates, gates_t, bias_row, bias_col, *, batch, seq, heads, head_dim):
    n = qk.shape[0]
    L = ML_CHUNK
    nc = seq // L
    width = heads * head_dim
    n_gates = gates_t.shape[0]
    fw = lambda b, c: b * nc + c
    bw = lambda b, c: b * nc + (nc - 1 - c)

    def specs(rowf):
        return [
            pl.BlockSpec((L, width), lambda b, c: (rowf(b, c), 0)),
            pl.BlockSpec((L, width), lambda b, c: (rowf(b, c), 1)),
            pl.BlockSpec((width, L), lambda b, c: (0, rowf(b, c))),
            pl.BlockSpec((L, LANES), lambda b, c: (rowf(b, c), 0)),
            pl.BlockSpec((n_gates, L), lambda b, c: (0, rowf(b, c))),
        ]

    def call_args():
        return [qk, qk, avt, gates, gates_t]

    return pl.pallas_call(
        functools.partial(_mlstm_body, heads=heads),
        grid=(batch, nc),
        in_specs=specs(fw) + specs(bw) + [
            pl.BlockSpec((1, LANES), lambda b, c: (0, 0)),
            pl.BlockSpec((n_gates, 1), lambda b, c: (0, 0)),
        ],
        out_specs=[pl.BlockSpec((L, width), lambda b, c: (fw(b, c), 0)),
                   pl.BlockSpec((L, width), lambda b, c: (bw(b, c), 0))],
        out_shape=[jax.ShapeDtypeStruct((n, width), F32)] * 2,
        scratch_shapes=[
            pltpu.VMEM((2 * heads, head_dim + ONES_ROWS, head_dim), F32),
            pltpu.VMEM((2 * heads, 1, 1), F32),
        ],
        compiler_params=_params(2),
        name="mlstm",
    )(*call_args(), *call_args(), bias_row, bias_col)


def _attn_body(lam_ref, g_ref, q_ref, k_ref, vt_ref, bz_ref, o_ref, m_scr, acc_scr,
               s_scr0, s_scr1, p_scr0, p_scr1, a_scr0, a_scr1, x_scr0, x_scr1, *, lambda_init):
    tq = q_ref.shape[0]
    n_chunks, d_v, tk = vt_ref.shape
    q = q_ref[...]
    lane = lax.broadcasted_iota(jnp.int32, (tq, LANES), 1)
    zero = jnp.zeros_like(q)
    q_maps = (jnp.where(lane < LANES // 2, q, zero), jnp.where(lane >= LANES // 2, q, zero))
    s_scr, p_scr, a_scr = (s_scr0, s_scr1), (p_scr0, p_scr1), (a_scr0, a_scr1)
    x_scr = (x_scr0, x_scr1)
    ones_rows = jnp.ones((acc_scr.shape[1] - d_v, tk), BF16)

    m_scr[...] = jnp.full(m_scr.shape, -jnp.inf, F32)
    acc_scr[...] = jnp.zeros(acc_scr.shape, F32)

    def scores(j, slot):
        k = k_ref[pl.ds(pl.multiple_of(j * tk, tk), tk), :]
        for mi in range(2):
            s = lax.dot_general(k, q_maps[mi], _NT, preferred_element_type=F32)
            s_scr[slot][mi, :, :tq] = s
            x_scr[slot][mi] = jnp.max(s, axis=0, keepdims=True)

    def softmax(slot):
        for mi in range(2):
            m_prev = m_scr[mi]
            m_cur = jnp.maximum(m_prev, x_scr[slot][mi])
            m_scr[mi] = m_cur
            a_scr[slot][mi] = jnp.exp2(m_prev - m_cur)
            p_scr[slot][mi, :, :tq] = jnp.exp2(s_scr[slot][mi, :, :tq] - m_cur).astype(BF16)

    def values(j, slot):
        vt = jnp.concatenate([vt_ref[j], ones_rows], axis=0)
        for mi in range(2):
            acc_scr[mi] = a_scr[slot][mi] * acc_scr[mi] + jnp.dot(
                vt, p_scr[slot][mi, :, :tq], preferred_element_type=F32)

    scores(0, 0)
    scores(1, 1)
    softmax(0)

    def two_steps(t, carry):
        j = 2 * t + 1
        scores(j + 1, 0)
        softmax(1)
        values(j - 1, 0)
        scores(j + 2, 1)
        softmax(0)
        values(j, 1)
        return carry

    lax.fori_loop(0, (n_chunks - 2) // 2, two_steps, 0)
    softmax(1)
    values(n_chunks - 2, 0)
    values(n_chunks - 1, 1)

    lp = lam_ref[...]
    dots = jnp.sum(lp[0:1, :] * lp[1:2, :], axis=-1, keepdims=True), \
        jnp.sum(lp[2:3, :] * lp[3:4, :], axis=-1, keepdims=True)
    lam = jnp.exp(dots[0]) - jnp.exp(dots[1]) + lambda_init
    acc0, acc1 = acc_scr[0], acc_scr[1]
    att = (acc0[:d_v] / acc0[d_v:d_v + 1] - lam * (acc1[:d_v] / acc1[d_v:d_v + 1]))
    ms = jnp.mean(att * att, axis=0, keepdims=True)
    att = (att * lax.rsqrt(ms + NORM_EPS)).T
    att = att * g_ref[...] * (1.0 - lambda_init)
    o_ref[...] = (att * _silu(bz_ref[...].astype(F32))).astype(BF16)


def _diff_attn(lam_p, subln_g, rq, rk, vt, p, *, batch, seq, heads, z_col, tq, lambda_init):
    n = rq.shape[0]
    nq = seq // tq
    tk = vt.shape[2]
    nk = seq // tk
    body = functools.partial(_attn_body, lambda_init=lambda_init)
    return pl.pallas_call(
        body,
        grid=(batch, heads, nq),
        in_specs=[
            pl.BlockSpec(lam_p.shape, lambda b, h, i: (0, 0)),
            pl.BlockSpec((1, LANES), lambda b, h, i: (0, 0)),
            pl.BlockSpec((tq, LANES), lambda b, h, i: (b * nq + i, h)),
            pl.BlockSpec((seq, LANES), lambda b, h, i: (b, h)),
            pl.BlockSpec((nk, LANES, tk), lambda b, h, i: (b, h, 0)),
            pl.BlockSpec((tq, LANES), lambda b, h, i: (b * nq + i, z_col + h)),
        ],
        out_specs=pl.BlockSpec((tq, LANES), lambda b, h, i: (b * nq + i, h)),
        out_shape=jax.ShapeDtypeStruct((n, heads * LANES), BF16),
        scratch_shapes=[
            pltpu.VMEM((2, 1, tq), F32),
            pltpu.VMEM((2, LANES + ONES_ROWS, tq), F32),
            pltpu.VMEM((2, tk, tq + LANES), F32), pltpu.VMEM((2, tk, tq + LANES), F32),
            pltpu.VMEM((2, tk, tq + LANES), BF16), pltpu.VMEM((2, tk, tq + LANES), BF16),
            pltpu.VMEM((2, 1, tq), F32), pltpu.VMEM((2, 1, tq), F32),
            pltpu.VMEM((2, 1, tq), F32), pltpu.VMEM((2, 1, tq), F32),
        ],
        compiler_params=_params(3),
        name="diff_attn",
    )(lam_p, subln_g, rq, rk, vt, p)


def _merge_body(hf_ref, hb_ref, ao_ref, az_ref, yb_ref, ga_ref, gb_ref, x_ref, mlg_ref, gbias_ref,
                fing_ref, wa_ref, wb_ref, wo_ref, o_ref, *, head_dim, final_norm):
    hs = hf_ref[...] + hb_ref[...]
    width = hs.shape[1]
    parts = []
    for c0 in range(0, width, head_dim):
        hh = hs[:, c0:c0 + head_dim]
        ms = jnp.mean(hh * hh, axis=-1, keepdims=True)
        parts.append(hh * lax.rsqrt(ms + NORM_EPS))
    hm = jnp.concatenate(parts, axis=-1) * mlg_ref[...]
    y_a = hm * _sigmoid(ao_ref[...].astype(F32)) * _silu(az_ref[...].astype(F32))
    pa = jnp.dot(y_a.astype(BF16), wa_ref[...], preferred_element_type=F32)
    pb = jnp.dot(yb_ref[...], wb_ref[...], preferred_element_type=F32)
    gbias = gbias_ref[...]
    mix = (_sigmoid(ga_ref[...].astype(F32) + gbias[0:1, :]) * pa
           + _sigmoid(gb_ref[...].astype(F32) + gbias[1:2, :]) * pb)
    xn = x_ref[...] + jnp.dot(mix.astype(BF16), wo_ref[...], preferred_element_type=F32)
    if final_norm:
        ms = jnp.mean(xn * xn, axis=-1, keepdims=True)
        xn = xn * lax.rsqrt(ms + NORM_EPS) * fing_ref[...]
    o_ref[...] = xn


def _merge(h_fw, h_bw, p, y_b, x2, ml_g, gate_bias, final_g, w_a, w_b, w_o, *, tm, head_dim,
           o_col, z_col, ga_col, gb_col, final_norm):
    n, d = x2.shape
    row = lambda c: pl.BlockSpec((tm, d), lambda i: (i, c))
    const = lambda shape: pl.BlockSpec(shape, lambda i: (0, 0))
    body = functools.partial(_merge_body, head_dim=head_dim, final_norm=final_norm)
    return pl.pallas_call(
        body,
        grid=(n // tm,),
        in_specs=[row(0), row(0), row(o_col), row(z_col), row(0), row(ga_col), row(gb_col), row(0),
                  const((1, d)), const((2, d)), const((1, d)),
                  const((d, d)), const((d, d)), const((d, d))],
        out_specs=row(0),
        out_shape=jax.ShapeDtypeStruct((n, d), F32),
        compiler_params=_params(1),
        name="merge",
    )(h_fw, h_bw, p, p, y_b, p, p, x2, ml_g, gate_bias, final_g, w_a, w_b, w_o)


def kernel(x, positions, norm_g, w_in, ml_gate_b, ml_conv_w, ml_norm_g, da_lambda, da_subln_g,
           gate_b, w_branch_a, w_branch_b, w_out, final_g):
    batch, seq, d = x.shape
    depth = w_in.shape[0]
    ml_heads, ml_head_dim = ml_norm_g.shape[1], ml_norm_g.shape[2]
    ml_width = ml_heads * ml_head_dim
    n_gates = 4 * ml_heads
    da_v_dim = da_subln_g.shape[1]
    da_qk_dim = da_lambda.shape[2]
    da_width = w_branch_b.shape[1]
    da_heads = da_width // da_v_dim
    assert ml_width == d and da_width == d and da_v_dim == LANES and 2 * da_qk_dim == LANES
    assert w_in.shape[2] == 11 * d + n_gates and seq % ML_CHUNK == 0
    n = batch * seq

    col = {name: i for i, name in enumerate(
        ("a_q", "a_k", "a_o", "a_z", "b_q", "b_k", "b_z", "g_a", "g_b"))}
    bv0 = 7 * d + n_gates

    tm_proj = min(1024, seq)
    tm_prep = min(512, seq)
    tq = min(512, seq)
    tk = min(512, seq)
    tm_merge = min(256, seq)

    x2 = x.reshape(n, d)
    pos2 = positions.reshape(n, 1)
    inv_freq = ROPE_THETA ** (-jnp.arange(0, da_qk_dim, 2, dtype=F32) / da_qk_dim)
    freq_tile = jnp.tile(inv_freq, LANES // inv_freq.shape[0]).reshape(1, LANES)

    for layer in range(depth):
        w = w_in[layer]
        w_main = jnp.concatenate([w[:, :2 * d], w[:, 3 * d:5 * d], w[:, 5 * d + n_gates:bv0],
                                  w[:, bv0 + d:]], axis=1).astype(BF16)
        wvt = w[:, bv0:bv0 + d].T.astype(BF16)
        wavt = w[:, 2 * d:3 * d].T.astype(BF16)
        w_g = w[:, 5 * d:5 * d + n_gates].reshape(d, 4, ml_heads).transpose(0, 2, 1)
        w_g = w_g.reshape(d, n_gates).astype(BF16)
        wg = jnp.pad(w_g, ((0, 0), (0, LANES - n_gates)))
        bias_hg = ml_gate_b[layer].astype(F32).T
        bias_row = jnp.pad(bias_hg.reshape(1, n_gates), ((0, 0), (0, LANES - n_gates)))
        bias_col = bias_hg.reshape(n_gates, 1)

        p, gates, gates_t, vt, avt = _in_proj(x2, norm_g[layer].reshape(1, d), w_main, wg, w_g.T,
                                              wvt, wavt, tm_proj, d, tk)
        qk, rq, rk = _prep(p, pos2, freq_tile, ml_conv_w[layer].astype(F32), seq=seq,
                           ml_width=ml_width, da_width=da_width, bq_col=col["b_q"], tm=tm_prep,
                           k_scale=ml_head_dim ** -0.5,
                           q_scale=da_qk_dim ** -0.5 * math.log2(math.e))
        h_fw, h_bw = _mlstm(qk, avt, gates, gates_t, bias_row, bias_col,
                            batch=batch, seq=seq, heads=ml_heads, head_dim=ml_head_dim)
        lambda_init = 0.8 - 0.6 * math.exp(-0.3 * layer)
        y_b = _diff_attn(da_lambda[layer].astype(F32), da_subln_g[layer].reshape(1, LANES), rq, rk,
                         vt, p, batch=batch, seq=seq, heads=da_heads,
                         z_col=col["b_z"] * da_heads, tq=tq, lambda_init=lambda_init)
        x2 = _merge(h_fw, h_bw, p, y_b, x2, ml_norm_g[layer].reshape(1, ml_width), gate_b[layer],
                    final_g.reshape(1, d), w_branch_a[layer].astype(BF16),
                    w_branch_b[layer].astype(BF16), w_out[layer].astype(BF16), tm=tm_merge,
                    head_dim=ml_head_dim, o_col=col["a_o"], z_col=col["a_z"], ga_col=col["g_a"],
                    gb_col=col["g_b"], final_norm=(layer == depth - 1))
    return x2.reshape(batch, seq, d)
```

```python
import functools
import math

import jax
import jax.numpy as jnp
from jax import lax
from jax.experimental import pallas as pl
from jax.experimental.pallas import tpu as pltpu

F32 = jnp.float32
BF16 = jnp.bfloat16

NORM_EPS = 1e-6
ROPE_THETA = 10000.0
ML_CHUNK = 128
LANES = 128
SUBLANES = 8
ONES_ROWS = 16
CONV_ROWS = 64
ATTN_UNROLL_PAIRS = 3
VMEM_LIMIT_BYTES = 56 * 1024 * 1024

_NT = (((1,), (1,)), ((), ()))
_TN = (((0,), (0,)), ((), ()))


def _params(n_axes):
    return pltpu.CompilerParams(
        dimension_semantics=("arbitrary",) * n_axes,
        vmem_limit_bytes=VMEM_LIMIT_BYTES)


def _sigmoid(x):
    return 1.0 / (1.0 + jnp.exp(-x))


def _silu(x):
    return x * _sigmoid(x)


def _log_sigmoid(x):
    return jnp.minimum(x, 0.0) - jnp.log1p(jnp.exp(-jnp.abs(x)))


def _rope_tables(pos_ref, freq_ref):
    tm = pos_ref.shape[0]
    d_map = LANES // 2
    n_freq = d_map // 2
    groups = LANES // n_freq
    rows_g = tm // groups
    lane_c = lax.broadcasted_iota(jnp.int32, (rows_g, LANES), 1)
    pos_c = pos_ref[0:rows_g, :]
    for g in range(1, groups):
        pos_c = jnp.where(lane_c >= g * n_freq, pos_ref[g * rows_g:(g + 1) * rows_g, :], pos_c)
    ang_c = pos_c.astype(F32) * freq_ref[...]

    def spread(table_c):
        parts = []
        for g in range(groups):
            base = table_c if g == 0 else pltpu.roll(table_c, LANES - g * n_freq, 1)
            full = base
            for r in range(1, groups):
                full = jnp.where(lane_c >= r * n_freq, pltpu.roll(base, r * n_freq, 1), full)
            parts.append(full)
        return jnp.concatenate(parts, axis=0)

    lane = lax.broadcasted_iota(jnp.int32, (tm, LANES), 1)
    sin = spread(jnp.sin(ang_c))
    return spread(jnp.cos(ang_c)), jnp.where((lane % d_map) < n_freq, -sin, sin)


def _in_proj_body(x_ref, g_ref, w_ref, wg_ref, wgt_ref, wvt_ref, wavt_ref, pos_ref, freq_ref,
                  p_ref, gates_ref, gatest_ref, vt_ref, avt_ref, h_scr, cos_scr, sin_scr,
                  *, rope_cols, q_scale):
    j = pl.program_id(1)

    @pl.when(j == 0)
    def _():
        cos_scr[...], sin_scr[...] = _rope_tables(pos_ref, freq_ref)
        x = x_ref[...]
        ms = jnp.mean(x * x, axis=-1, keepdims=True)
        h = (x * lax.rsqrt(ms + NORM_EPS) * g_ref[...]).astype(BF16)
        h_scr[...] = h
        gates_ref[...] = jnp.dot(h, wg_ref[...], preferred_element_type=F32)
        gatest_ref[...] = lax.dot_general(wgt_ref[...], h, _NT, preferred_element_type=F32)
        avt_ref[...] = lax.dot_general(wavt_ref[...], h, _NT,
                                       preferred_element_type=F32).astype(BF16)
        vt = lax.dot_general(wvt_ref[...], h, _NT, preferred_element_type=F32).astype(BF16)
        tk = vt_ref.shape[2]
        for c in range(vt_ref.shape[0]):
            vt_ref[c] = vt[:, c * tk:(c + 1) * tk]

    is_rope = functools.reduce(jnp.logical_or, [j == c for c in rope_cols])

    @pl.when(jnp.logical_not(is_rope))
    def _():
        p_ref[...] = jnp.dot(h_scr[...], w_ref[...], preferred_element_type=F32).astype(BF16)

    @pl.when(is_rope)
    def _():
        r = jnp.dot(h_scr[...], w_ref[...], preferred_element_type=F32)
        scale = jnp.where(j == rope_cols[0], q_scale, 1.0)
        cos = cos_scr[...] * scale
        sin = sin_scr[...] * scale
        lane = lax.broadcasted_iota(jnp.int32, cos.shape, 1)
        d_map = LANES // 2
        low = (lane % d_map) < (d_map // 2)
        for c0 in range(0, r.shape[1], LANES):
            t = r[:, c0:c0 + LANES]
            up = pltpu.roll(t, LANES - d_map // 2, 1)
            down = pltpu.roll(t, d_map // 2, 1)
            p_ref[:, c0:c0 + LANES] = (t * cos + jnp.where(low, up, down) * sin).astype(BF16)


def _in_proj(x2, norm_g, w_main, wg, wgt, wvt, wavt, pos2, freq_tile, *, tm, tn, tk, rope_cols,
             q_scale):
    n, d = x2.shape
    width = w_main.shape[1]
    ng = wgt.shape[0]
    dv = wvt.shape[0]
    dav = wavt.shape[0]
    return pl.pallas_call(
        functools.partial(_in_proj_body, rope_cols=rope_cols, q_scale=q_scale),
        grid=(n // tm, width // tn),
        in_specs=[
            pl.BlockSpec((tm, d), lambda i, j: (i, 0)),
            pl.BlockSpec((1, d), lambda i, j: (0, 0)),
            pl.BlockSpec((d, tn), lambda i, j: (0, j)),
            pl.BlockSpec((d, LANES), lambda i, j: (0, 0)),
            pl.BlockSpec((ng, d), lambda i, j: (0, 0)),
            pl.BlockSpec((dv, d), lambda i, j: (0, 0)),
            pl.BlockSpec((dav, d), lambda i, j: (0, 0)),
            pl.BlockSpec((tm, 1), lambda i, j: (i, 0)),
            pl.BlockSpec((1, LANES), lambda i, j: (0, 0)),
        ],
        out_specs=[
            pl.BlockSpec((tm, tn), lambda i, j: (i, j)),
            pl.BlockSpec((tm, LANES), lambda i, j: (i, 0)),
            pl.BlockSpec((ng, tm), lambda i, j: (0, i)),
            pl.BlockSpec((tm // tk, dv, tk), lambda i, j: (i, 0, 0)),
            pl.BlockSpec((dav, tm), lambda i, j: (0, i)),
        ],
        out_shape=[
            jax.ShapeDtypeStruct((n, width), BF16),
            jax.ShapeDtypeStruct((n, LANES), F32),
            jax.ShapeDtypeStruct((ng, n), F32),
            jax.ShapeDtypeStruct((n // tk, dv, tk), BF16),
            jax.ShapeDtypeStruct((dav, n), BF16),
        ],
        scratch_shapes=[pltpu.VMEM((tm, d), BF16), pltpu.VMEM((tm, LANES), F32),
                        pltpu.VMEM((tm, LANES), F32)],
        compiler_params=_params(2),
        name="in_proj",
    )(x2, norm_g, w_main, wg, wgt, wvt, wavt, pos2, freq_tile)


def _prep_body(qk_ref, prev_ref, next_ref, cw_ref, qk_out, ext_scr, *, tiles_per_seq, conv_width,
               k_scale):
    i = pl.program_id(0)
    tm = qk_ref.shape[0]
    width = qk_ref.shape[1]
    pad = conv_width // 2
    halo = SUBLANES
    t_in_seq = i % tiles_per_seq
    has_prev = (t_in_seq > 0).astype(F32)
    has_next = (t_in_seq < tiles_per_seq - 1).astype(F32)

    slab = ext_scr.shape[1]
    for c0 in range(0, width, slab):
        cols = slice(c0, c0 + slab)
        ext_scr[0:halo, :] = prev_ref[:, cols].astype(F32) * has_prev
        ext_scr[halo:halo + tm, :] = qk_ref[:, cols].astype(F32)
        ext_scr[halo + tm:halo + tm + halo, :] = next_ref[:, cols].astype(F32) * has_next
        scale = 1.0 if c0 < width // 2 else k_scale
        taps = [cw_ref[t:t + 1, cols] for t in range(conv_width)]
        for r0 in range(0, tm, CONV_ROWS):
            acc = None
            for t in range(conv_width):
                term = ext_scr[pl.ds(r0 + halo - pad + t, CONV_ROWS), :] * taps[t]
                acc = term if acc is None else acc + term
            qk_out[r0:r0 + CONV_ROWS, cols] = (_silu(acc) * scale).astype(BF16)


def _prep(p, conv_w, *, seq, ml_width, tm, k_scale):
    n = p.shape[0]
    n_tiles = n // tm
    halo_blocks = n // SUBLANES
    qk_w = 2 * ml_width
    body = functools.partial(
        _prep_body, tiles_per_seq=seq // tm, conv_width=conv_w.shape[0], k_scale=k_scale)
    per_tile = tm // SUBLANES
    return pl.pallas_call(
        body,
        grid=(n_tiles,),
        in_specs=[
            pl.BlockSpec((tm, qk_w), lambda i: (i, 0)),
            pl.BlockSpec((SUBLANES, qk_w), lambda i: (jnp.maximum(i * per_tile - 1, 0), 0)),
            pl.BlockSpec((SUBLANES, qk_w),
                         lambda i: (jnp.minimum((i + 1) * per_tile, halo_blocks - 1), 0)),
            pl.BlockSpec((conv_w.shape[0], qk_w), lambda i: (0, 0)),
        ],
        out_specs=pl.BlockSpec((tm, qk_w), lambda i: (i, 0)),
        out_shape=jax.ShapeDtypeStruct((n, qk_w), BF16),
        scratch_shapes=[pltpu.VMEM((tm + 2 * SUBLANES, 256), F32)],
        compiler_params=_params(1),
        name="prep",
    )(p, p, p, conv_w)


def _split3(f):
    hi = f.astype(BF16)
    r1 = f - hi.astype(F32)
    mid = r1.astype(BF16)
    lo = (r1 - mid.astype(F32)).astype(BF16)
    return hi, mid, lo


def _mlstm_chunk(qc, kc, vt, i_row, b_row, a_col, valid_t, c_ref, m_ref, forward):
    L = qc.shape[0]
    d = vt.shape[0]
    a_row = i_row - b_row
    m_st = m_ref[...]
    log_d = jnp.where(valid_t, a_col + b_row, -jnp.inf)
    m_t = jnp.maximum(b_row + m_st, jnp.max(log_d, axis=0, keepdims=True))
    w_intra = lax.dot_general(kc, qc, _NT, preferred_element_type=F32) * jnp.exp(log_d - m_t)
    s_inter = jnp.exp(b_row + m_st - m_t)
    c_aug = c_ref[...]
    inter = lax.dot_general(c_aug.astype(BF16), qc, _NT, preferred_element_type=F32)
    num = jnp.dot(vt, w_intra.astype(BF16), preferred_element_type=F32) + s_inter * inter[:d]
    den = jnp.sum(w_intra, axis=0, keepdims=True) + s_inter * inter[d:d + 1]
    h_t = num / jnp.maximum(jnp.abs(den), jnp.exp(-m_t))

    b_last = b_row[:, L - 1:L] if forward else b_row[:, 0:1]
    m_new = jnp.maximum(b_last + m_st, jnp.max(b_last + a_row, axis=-1, keepdims=True))
    w_row = jnp.exp(b_last + a_row - m_new)
    decay = jnp.exp(b_last + m_st - m_new)
    lhs = jnp.concatenate([vt.astype(F32) * w_row, jnp.broadcast_to(w_row, (ONES_ROWS, L))], axis=0)
    c_ref[...] = decay * c_aug + jnp.dot(lhs.astype(BF16), kc, preferred_element_type=F32)
    m_ref[...] = m_new
    return h_t.T


def _mlstm_body(qf_ref, kf_ref, vf_ref, gf_ref, gtf_ref, qb_ref, kb_ref, vb_ref, gb_ref, gtb_ref,
                brow_ref, bcol_ref, hf_ref, hb_ref, c_scr, m_scr, *, heads):
    @pl.when(pl.program_id(1) == 0)
    def _():
        for ref in (c_scr, m_scr):
            ref[...] = jnp.zeros(ref.shape, ref.dtype)

    L = qf_ref.shape[0]
    d = qf_ref.shape[1] // heads
    r = lax.broadcasted_iota(jnp.int32, (L, L), 0)
    c = lax.broadcasted_iota(jnp.int32, (L, L), 1)
    lower, upper = c <= r, c >= r
    tri = {True: lower.astype(BF16), False: upper.astype(BF16)}

    for forward, q_ref, k_ref, v_ref, g_ref, gt_ref, out_ref in (
            (True, qf_ref, kf_ref, vf_ref, gf_ref, gtf_ref, hf_ref),
            (False, qb_ref, kb_ref, vb_ref, gb_ref, gtb_ref, hb_ref)):
        gi = 0 if forward else 2
        g = g_ref[...] + brow_ref[...]
        gt = gt_ref[...] + bcol_ref[...]
        cum_col = sum(jnp.dot(tri[forward], part, preferred_element_type=F32)
                      for part in _split3(_log_sigmoid(g)))
        cum_row = sum(jnp.dot(part, tri[not forward], preferred_element_type=F32)
                      for part in _split3(_log_sigmoid(gt)))
        valid_t = upper if forward else lower
        for h in range(heads):
            ii, fi = h * 4 + gi, h * 4 + gi + 1
            cols = slice(h * d, (h + 1) * d)
            ci = h if forward else heads + h
            out_ref[:, cols] = _mlstm_chunk(
                q_ref[:, cols], k_ref[:, cols], v_ref[cols, :],
                gt[ii:ii + 1, :], cum_row[fi:fi + 1, :], g[:, ii:ii + 1] - cum_col[:, fi:fi + 1],
                valid_t, c_scr.at[ci], m_scr.at[ci], forward)


def _mlstm(qk, avt, gates, gates_t, bias_row, bias_col, *, batch, seq, heads, head_dim):
    n = qk.shape[0]
    L = ML_CHUNK
    nc = seq // L
    width = heads * head_dim
    n_gates = gates_t.shape[0]
    fw = lambda b, c: b * nc + c
    bw = lambda b, c: b * nc + (nc - 1 - c)

    def specs(rowf):
        return [
            pl.BlockSpec((L, width), lambda b, c: (rowf(b, c), 0)),
            pl.BlockSpec((L, width), lambda b, c: (rowf(b, c), 1)),
            pl.BlockSpec((width, L), lambda b, c: (0, rowf(b, c))),
            pl.BlockSpec((L, LANES), lambda b, c: (rowf(b, c), 0)),
            pl.BlockSpec((n_gates, L), lambda b, c: (0, rowf(b, c))),
        ]

    def call_args():
        return [qk, qk, avt, gates, gates_t]

    return pl.pallas_call(
        functools.partial(_mlstm_body, heads=heads),
        grid=(batch, nc),
        in_specs=specs(fw) + specs(bw) + [
            pl.BlockSpec((1, LANES), lambda b, c: (0, 0)),
            pl.BlockSpec((n_gates, 1), lambda b, c: (0, 0)),
        ],
        out_specs=[pl.BlockSpec((L, width), lambda b, c: (fw(b, c), 0)),
                   pl.BlockSpec((L, width), lambda b, c: (bw(b, c), 0))],
        out_shape=[jax.ShapeDtypeStruct((n, width), F32)] * 2,
        scratch_shapes=[
            pltpu.VMEM((2 * heads, head_dim + ONES_ROWS, head_dim), F32),
            pltpu.VMEM((2 * heads, 1, 1), F32),
        ],
        compiler_params=_params(2),
        name="mlstm",
    )(*call_args(), *call_args(), bias_row, bias_col)


def _attn_body(lam_ref, g_ref, q_ref, k_ref, vt_ref, bz_ref, o_ref, m_scr, acc_scr,
               s_scr0, s_scr1, p_scr0, p_scr1, a_scr0, a_scr1, x_scr0, x_scr1, *, lambda_init):
    tq = q_ref.shape[0]
    n_chunks, d_v, tk = vt_ref.shape
    q = q_ref[...]
    lane = lax.broadcasted_iota(jnp.int32, (tq, LANES), 1)
    zero = jnp.zeros_like(q)
    q_maps = (jnp.where(lane < LANES // 2, q, zero), jnp.where(lane >= LANES // 2, q, zero))
    s_scr, p_scr, a_scr = (s_scr0, s_scr1), (p_scr0, p_scr1), (a_scr0, a_scr1)
    x_scr = (x_scr0, x_scr1)
    ones_rows = jnp.ones((acc_scr.shape[1] - d_v, tk), BF16)

    m_scr[...] = jnp.full(m_scr.shape, -jnp.inf, F32)
    acc_scr[...] = jnp.zeros(acc_scr.shape, F32)

    def scores(j, slot):
        k = k_ref[pl.ds(pl.multiple_of(j * tk, tk), tk), :]
        for mi in range(2):
            s = lax.dot_general(k, q_maps[mi], _NT, preferred_element_type=F32)
            s_scr[slot][mi, :, :tq] = s
            x_scr[slot][mi] = jnp.max(s, axis=0, keepdims=True)

    def softmax(slot):
        for mi in range(2):
            m_prev = m_scr[mi]
            m_cur = jnp.maximum(m_prev, x_scr[slot][mi])
            m_scr[mi] = m_cur
            a_scr[slot][mi] = jnp.exp2(m_prev - m_cur)
            p_scr[slot][mi, :, :tq] = jnp.exp2(s_scr[slot][mi, :, :tq] - m_cur).astype(BF16)

    def values(j, slot):
        vt = jnp.concatenate([vt_ref[j], ones_rows], axis=0)
        for mi in range(2):
            acc_scr[mi] = a_scr[slot][mi] * acc_scr[mi] + jnp.dot(
                vt, p_scr[slot][mi, :, :tq], preferred_element_type=F32)

    scores(0, 0)
    scores(1, 1)
    softmax(0)

    def two_steps(t):
        j = 2 * t + 1
        scores(j + 1, 0)
        softmax(1)
        values(j - 1, 0)
        scores(j + 2, 1)
        softmax(0)
        values(j, 1)

    n_pairs = (n_chunks - 2) // 2
    unroll = min(ATTN_UNROLL_PAIRS, max(n_pairs, 1))
    head = n_pairs % unroll
    for t in range(head):
        two_steps(t)

    def loop_body(i, carry):
        for u in range(unroll):
            two_steps(head + i * unroll + u)
        return carry

    lax.fori_loop(0, n_pairs // unroll, loop_body, 0)
    softmax(1)
    values(n_chunks - 2, 0)
    values(n_chunks - 1, 1)

    lp = lam_ref[...]
    dots = jnp.sum(lp[0:1, :] * lp[1:2, :], axis=-1, keepdims=True), \
        jnp.sum(lp[2:3, :] * lp[3:4, :], axis=-1, keepdims=True)
    lam = jnp.exp(dots[0]) - jnp.exp(dots[1]) + lambda_init
    acc0, acc1 = acc_scr[0], acc_scr[1]
    att = (acc0[:d_v] / acc0[d_v:d_v + 1] - lam * (acc1[:d_v] / acc1[d_v:d_v + 1]))
    ms = jnp.mean(att * att, axis=0, keepdims=True)
    att = (att * lax.rsqrt(ms + NORM_EPS)).T
    att = att * g_ref[...] * (1.0 - lambda_init)
    o_ref[...] = (att * _silu(bz_ref[...].astype(F32))).astype(BF16)


def _diff_attn(lam_p, subln_g, vt, p, *, batch, seq, heads, q_col, k_col, z_col, tq, lambda_init):
    n = p.shape[0]
    nq = seq // tq
    tk = vt.shape[2]
    nk = seq // tk
    body = functools.partial(_attn_body, lambda_init=lambda_init)
    return pl.pallas_call(
        body,
        grid=(batch, heads, nq),
        in_specs=[
            pl.BlockSpec(lam_p.shape, lambda b, h, i: (0, 0)),
            pl.BlockSpec((1, LANES), lambda b, h, i: (0, 0)),
            pl.BlockSpec((tq, LANES), lambda b, h, i: (b * nq + i, q_col + h)),
            pl.BlockSpec((seq, LANES), lambda b, h, i: (b, k_col + h)),
            pl.BlockSpec((nk, LANES, tk), lambda b, h, i: (b, h, 0)),
            pl.BlockSpec((tq, LANES), lambda b, h, i: (b * nq + i, z_col + h)),
        ],
        out_specs=pl.BlockSpec((tq, LANES), lambda b, h, i: (b * nq + i, h)),
        out_shape=jax.ShapeDtypeStruct((n, heads * LANES), BF16),
        scratch_shapes=[
            pltpu.VMEM((2, 1, tq), F32),
            pltpu.VMEM((2, LANES + ONES_ROWS, tq), F32),
            pltpu.VMEM((2, tk, tq + LANES), F32), pltpu.VMEM((2, tk, tq + LANES), F32),
            pltpu.VMEM((2, tk, tq + LANES), BF16), pltpu.VMEM((2, tk, tq + LANES), BF16),
            pltpu.VMEM((2, 1, tq), F32), pltpu.VMEM((2, 1, tq), F32),
            pltpu.VMEM((2, 1, tq), F32), pltpu.VMEM((2, 1, tq), F32),
        ],
        compiler_params=_params(3),
        name="diff_attn",
    )(lam_p, subln_g, p, p, vt, p)


def _merge_body(hf_ref, hb_ref, ao_ref, az_ref, yb_ref, ga_ref, gb_ref, x_ref, mlg_ref, gbias_ref,
                fing_ref, wa_ref, wb_ref, wo_ref, o_ref, *, head_dim, final_norm):
    hs = hf_ref[...] + hb_ref[...]
    width = hs.shape[1]
    parts = []
    for c0 in range(0, width, head_dim):
        hh = hs[:, c0:c0 + head_dim]
        ms = jnp.mean(hh * hh, axis=-1, keepdims=True)
        parts.append(hh * lax.rsqrt(ms + NORM_EPS))
    hm = jnp.concatenate(parts, axis=-1) * mlg_ref[...]
    y_a = hm * _sigmoid(ao_ref[...].astype(F32)) * _silu(az_ref[...].astype(F32))
    pa = jnp.dot(y_a.astype(BF16), wa_ref[...], preferred_element_type=F32)
    pb = jnp.dot(yb_ref[...], wb_ref[...], preferred_element_type=F32)
    gbias = gbias_ref[...]
    mix = (_sigmoid(ga_ref[...].astype(F32) + gbias[0:1, :]) * pa
           + _sigmoid(gb_ref[...].astype(F32) + gbias[1:2, :]) * pb)
    xn = x_ref[...] + jnp.dot(mix.astype(BF16), wo_ref[...], preferred_element_type=F32)
    if final_norm:
        ms = jnp.mean(xn * xn, axis=-1, keepdims=True)
        xn = xn * lax.rsqrt(ms + NORM_EPS) * fing_ref[...]
    o_ref[...] = xn


def _merge(h_fw, h_bw, p, y_b, x2, ml_g, gate_bias, final_g, w_a, w_b, w_o, *, tm, head_dim,
           o_col, z_col, ga_col, gb_col, final_norm):
    n, d = x2.shape
    row = lambda c: pl.BlockSpec((tm, d), lambda i: (i, c))
    const = lambda shape: pl.BlockSpec(shape, lambda i: (0, 0))
    body = functools.partial(_merge_body, head_dim=head_dim, final_norm=final_norm)
    return pl.pallas_call(
        body,
        grid=(n // tm,),
        in_specs=[row(0), row(0), row(o_col), row(z_col), row(0), row(ga_col), row(gb_col), row(0),
                  const((1, d)), const((2, d)), const((1, d)),
                  const((d, d)), const((d, d)), const((d, d))],
        out_specs=row(0),
        out_shape=jax.ShapeDtypeStruct((n, d), F32),
        compiler_params=_params(1),
        name="merge",
    )(h_fw, h_bw, p, p, y_b, p, p, x2, ml_g, gate_bias, final_g, w_a, w_b, w_o)


def kernel(x, positions, norm_g, w_in, ml_gate_b, ml_conv_w, ml_norm_g, da_lambda, da_subln_g,
           gate_b, w_branch_a, w_branch_b, w_out, final_g):
    batch, seq, d = x.shape
    depth = w_in.shape[0]
    ml_heads, ml_head_dim = ml_norm_g.shape[1], ml_norm_g.shape[2]
    ml_width = ml_heads * ml_head_dim
    n_gates = 4 * ml_heads
    da_v_dim = da_subln_g.shape[1]
    da_qk_dim = da_lambda.shape[2]
    da_width = w_branch_b.shape[1]
    da_heads = da_width // da_v_dim
    assert ml_width == d and da_width == d and da_v_dim == LANES and 2 * da_qk_dim == LANES
    assert w_in.shape[2] == 11 * d + n_gates and seq % ML_CHUNK == 0
    n = batch * seq

    col = {name: i for i, name in enumerate(
        ("a_q", "a_k", "a_o", "a_z", "b_q", "b_k", "b_z", "g_a", "g_b"))}
    bv0 = 7 * d + n_gates

    tm_proj = min(1024, seq)
    tm_prep = min(512, seq)
    tq = min(512, seq)
    tk = min(512, seq)
    tm_merge = min(256, seq)

    x2 = x.reshape(n, d)
    pos2 = positions.reshape(n, 1)
    inv_freq = ROPE_THETA ** (-jnp.arange(0, da_qk_dim, 2, dtype=F32) / da_qk_dim)
    freq_tile = jnp.tile(inv_freq, LANES // inv_freq.shape[0]).reshape(1, LANES)

    for layer in range(depth):
        w = w_in[layer]
        w_main = jnp.concatenate([w[:, :2 * d], w[:, 3 * d:5 * d], w[:, 5 * d + n_gates:bv0],
                                  w[:, bv0 + d:]], axis=1).astype(BF16)
        wvt = w[:, bv0:bv0 + d].T.astype(BF16)
        wavt = w[:, 2 * d:3 * d].T.astype(BF16)
        w_g = w[:, 5 * d:5 * d + n_gates].reshape(d, 4, ml_heads).transpose(0, 2, 1)
        w_g = w_g.reshape(d, n_gates).astype(BF16)
        wg = jnp.pad(w_g, ((0, 0), (0, LANES - n_gates)))
        bias_hg = ml_gate_b[layer].astype(F32).T
        bias_row = jnp.pad(bias_hg.reshape(1, n_gates), ((0, 0), (0, LANES - n_gates)))
        bias_col = bias_hg.reshape(n_gates, 1)

        p, gates, gates_t, vt, avt = _in_proj(
            x2, norm_g[layer].reshape(1, d), w_main, wg, w_g.T, wvt, wavt, pos2, freq_tile,
            tm=tm_proj, tn=d, tk=tk, rope_cols=(col["b_q"], col["b_k"]),
            q_scale=da_qk_dim ** -0.5 * math.log2(math.e))
        qk = _prep(p, ml_conv_w[layer].astype(F32), seq=seq, ml_width=ml_width, tm=tm_prep,
                   k_scale=ml_head_dim ** -0.5)
        h_fw, h_bw = _mlstm(qk, avt, gates, gates_t, bias_row, bias_col,
                            batch=batch, seq=seq, heads=ml_heads, head_dim=ml_head_dim)
        lambda_init = 0.8 - 0.6 * math.exp(-0.3 * layer)
        y_b = _diff_attn(da_lambda[layer].astype(F32), da_subln_g[layer].reshape(1, LANES), vt, p,
                         batch=batch, seq=seq, heads=da_heads, q_col=col["b_q"] * da_heads,
                         k_col=col["b_k"] * da_heads, z_col=col["b_z"] * da_heads, tq=tq,
                         lambda_init=lambda_init)
        x2 = _merge(h_fw, h_bw, p, y_b, x2, ml_norm_g[layer].reshape(1, ml_width), gate_b[layer],
                    final_g.reshape(1, d), w_branch_a[layer].astype(BF16),
                    w_branch_b[layer].astype(BF16), w_out[layer].astype(BF16), tm=tm_merge,
                    head_dim=ml_head_dim, o_col=col["a_o"], z_col=col["a_z"], ga_col=col["g_a"],
                    gb_col=col["g_b"], final_norm=(layer == depth - 1))
    return x2.reshape(batch, seq, d)
```

```python
import functools
import math

import jax
import jax.numpy as jnp
from jax import lax
from jax.experimental import pallas as pl
from jax.experimental.pallas import tpu as pltpu

F32 = jnp.float32
BF16 = jnp.bfloat16

NORM_EPS = 1e-6
ROPE_THETA = 10000.0
ML_CHUNK = 256
LANES = 128
SUBLANES = 8
ONES_ROWS = 16
CONV_ROWS = 64
ATTN_UNROLL_PAIRS = 3
VMEM_LIMIT_BYTES = 56 * 1024 * 1024

_NT = (((1,), (1,)), ((), ()))
_TN = (((0,), (0,)), ((), ()))


def _params(n_axes):
    return pltpu.CompilerParams(
        dimension_semantics=("arbitrary",) * n_axes,
        vmem_limit_bytes=VMEM_LIMIT_BYTES)


def _sigmoid(x):
    return 1.0 / (1.0 + jnp.exp2(x * (-math.log2(math.e))))


def _silu(x):
    return x * _sigmoid(x)


def _log_sigmoid(x):
    return jnp.minimum(x, 0.0) - jnp.log1p(jnp.exp(-jnp.abs(x)))


def _rope_tables(pos_ref, freq_ref):
    tm = pos_ref.shape[0]
    d_map = LANES // 2
    n_freq = d_map // 2
    groups = LANES // n_freq
    rows_g = tm // groups
    lane_c = lax.broadcasted_iota(jnp.int32, (rows_g, LANES), 1)
    pos_c = pos_ref[0:rows_g, :]
    for g in range(1, groups):
        pos_c = jnp.where(lane_c >= g * n_freq, pos_ref[g * rows_g:(g + 1) * rows_g, :], pos_c)
    ang_c = pos_c.astype(F32) * freq_ref[...]

    def spread(table_c):
        parts = []
        for g in range(groups):
            base = table_c if g == 0 else pltpu.roll(table_c, LANES - g * n_freq, 1)
            full = base
            for r in range(1, groups):
                full = jnp.where(lane_c >= r * n_freq, pltpu.roll(base, r * n_freq, 1), full)
            parts.append(full)
        return jnp.concatenate(parts, axis=0)

    lane = lax.broadcasted_iota(jnp.int32, (tm, LANES), 1)
    sin = spread(jnp.sin(ang_c))
    return spread(jnp.cos(ang_c)), jnp.where((lane % d_map) < n_freq, -sin, sin)


def _in_proj_body(x_ref, g_ref, w_ref, wg_ref, wgt_ref, wvt_ref, wavt_ref, pos_ref, freq_ref,
                  p_ref, gates_ref, gatest_ref, vt_ref, avt_ref, h_scr, cos_scr, sin_scr,
                  *, rope_cols, q_scale):
    j = pl.program_id(1)

    @pl.when(j == 0)
    def _():
        cos_scr[...], sin_scr[...] = _rope_tables(pos_ref, freq_ref)
        x = x_ref[...]
        ms = jnp.mean(x * x, axis=-1, keepdims=True)
        h = (x * lax.rsqrt(ms + NORM_EPS) * g_ref[...]).astype(BF16)
        h_scr[...] = h
        gates_ref[...] = jnp.dot(h, wg_ref[...], preferred_element_type=F32)
        gatest_ref[...] = lax.dot_general(wgt_ref[...], h, _NT, preferred_element_type=F32)
        avt_ref[...] = lax.dot_general(wavt_ref[...], h, _NT,
                                       preferred_element_type=F32).astype(BF16)
        vt = lax.dot_general(wvt_ref[...], h, _NT, preferred_element_type=F32).astype(BF16)
        tk = vt_ref.shape[2]
        for c in range(vt_ref.shape[0]):
            vt_ref[c] = vt[:, c * tk:(c + 1) * tk]

    is_rope = functools.reduce(jnp.logical_or, [j == c for c in rope_cols])

    @pl.when(jnp.logical_not(is_rope))
    def _():
        p_ref[...] = jnp.dot(h_scr[...], w_ref[...], preferred_element_type=F32).astype(BF16)

    @pl.when(is_rope)
    def _():
        r = jnp.dot(h_scr[...], w_ref[...], preferred_element_type=F32)
        scale = jnp.where(j == rope_cols[0], q_scale, 1.0)
        cos = cos_scr[...] * scale
        sin = sin_scr[...] * scale
        lane = lax.broadcasted_iota(jnp.int32, cos.shape, 1)
        d_map = LANES // 2
        low = (lane % d_map) < (d_map // 2)
        for c0 in range(0, r.shape[1], LANES):
            t = r[:, c0:c0 + LANES]
            up = pltpu.roll(t, LANES - d_map // 2, 1)
            down = pltpu.roll(t, d_map // 2, 1)
            p_ref[:, c0:c0 + LANES] = (t * cos + jnp.where(low, up, down) * sin).astype(BF16)


def _in_proj(x2, norm_g, w_main, wg, wgt, wvt, wavt, pos2, freq_tile, *, tm, tn, tk, rope_cols,
             q_scale):
    n, d = x2.shape
    width = w_main.shape[1]
    ng = wgt.shape[0]
    dv = wvt.shape[0]
    dav = wavt.shape[0]
    return pl.pallas_call(
        functools.partial(_in_proj_body, rope_cols=rope_cols, q_scale=q_scale),
        grid=(n // tm, width // tn),
        in_specs=[
            pl.BlockSpec((tm, d), lambda i, j: (i, 0)),
            pl.BlockSpec((1, d), lambda i, j: (0, 0)),
            pl.BlockSpec((d, tn), lambda i, j: (0, j)),
            pl.BlockSpec((d, LANES), lambda i, j: (0, 0)),
            pl.BlockSpec((ng, d), lambda i, j: (0, 0)),
            pl.BlockSpec((dv, d), lambda i, j: (0, 0)),
            pl.BlockSpec((dav, d), lambda i, j: (0, 0)),
            pl.BlockSpec((tm, 1), lambda i, j: (i, 0)),
            pl.BlockSpec((1, LANES), lambda i, j: (0, 0)),
        ],
        out_specs=[
            pl.BlockSpec((tm, tn), lambda i, j: (i, j)),
            pl.BlockSpec((tm, LANES), lambda i, j: (i, 0)),
            pl.BlockSpec((ng, tm), lambda i, j: (0, i)),
            pl.BlockSpec((tm // tk, dv, tk), lambda i, j: (i, 0, 0)),
            pl.BlockSpec((dav, tm), lambda i, j: (0, i)),
        ],
        out_shape=[
            jax.ShapeDtypeStruct((n, width), BF16),
            jax.ShapeDtypeStruct((n, LANES), F32),
            jax.ShapeDtypeStruct((ng, n), F32),
            jax.ShapeDtypeStruct((n // tk, dv, tk), BF16),
            jax.ShapeDtypeStruct((dav, n), BF16),
        ],
        scratch_shapes=[pltpu.VMEM((tm, d), BF16), pltpu.VMEM((tm, LANES), F32),
                        pltpu.VMEM((tm, LANES), F32)],
        compiler_params=_params(2),
        name="in_proj",
    )(x2, norm_g, w_main, wg, wgt, wvt, wavt, pos2, freq_tile)


def _prep_body(qk_ref, prev_ref, next_ref, cw_ref, qk_out, ext_scr, *, tiles_per_seq, conv_width,
               k_scale):
    i = pl.program_id(0)
    tm = qk_ref.shape[0]
    width = qk_ref.shape[1]
    pad = conv_width // 2
    halo = SUBLANES
    t_in_seq = i % tiles_per_seq
    has_prev = (t_in_seq > 0).astype(F32)
    has_next = (t_in_seq < tiles_per_seq - 1).astype(F32)

    slab = ext_scr.shape[1]
    for c0 in range(0, width, slab):
        cols = slice(c0, c0 + slab)
        ext_scr[0:halo, :] = prev_ref[:, cols].astype(F32) * has_prev
        ext_scr[halo:halo + tm, :] = qk_ref[:, cols].astype(F32)
        ext_scr[halo + tm:halo + tm + halo, :] = next_ref[:, cols].astype(F32) * has_next
        scale = 1.0 if c0 < width // 2 else k_scale
        taps = [cw_ref[t:t + 1, cols] for t in range(conv_width)]
        for r0 in range(0, tm, CONV_ROWS):
            acc = None
            for t in range(conv_width):
                term = ext_scr[pl.ds(r0 + halo - pad + t, CONV_ROWS), :] * taps[t]
                acc = term if acc is None else acc + term
            qk_out[r0:r0 + CONV_ROWS, cols] = (_silu(acc) * scale).astype(BF16)


def _prep(p, conv_w, *, seq, ml_width, tm, k_scale):
    n = p.shape[0]
    n_tiles = n // tm
    halo_blocks = n // SUBLANES
    qk_w = 2 * ml_width
    body = functools.partial(
        _prep_body, tiles_per_seq=seq // tm, conv_width=conv_w.shape[0], k_scale=k_scale)
    per_tile = tm // SUBLANES
    return pl.pallas_call(
        body,
        grid=(n_tiles,),
        in_specs=[
            pl.BlockSpec((tm, qk_w), lambda i: (i, 0)),
            pl.BlockSpec((SUBLANES, qk_w), lambda i: (jnp.maximum(i * per_tile - 1, 0), 0)),
            pl.BlockSpec((SUBLANES, qk_w),
                         lambda i: (jnp.minimum((i + 1) * per_tile, halo_blocks - 1), 0)),
            pl.BlockSpec((conv_w.shape[0], qk_w), lambda i: (0, 0)),
        ],
        out_specs=pl.BlockSpec((tm, qk_w), lambda i: (i, 0)),
        out_shape=jax.ShapeDtypeStruct((n, qk_w), BF16),
        scratch_shapes=[pltpu.VMEM((tm + 2 * SUBLANES, 256), F32)],
        compiler_params=_params(1),
        name="prep",
    )(p, p, p, conv_w)


def _split3(f):
    hi = f.astype(BF16)
    r1 = f - hi.astype(F32)
    mid = r1.astype(BF16)
    lo = (r1 - mid.astype(F32)).astype(BF16)
    return hi, mid, lo


def _mlstm_chunk(qc, kc, vt, i_row, b_row, a_col, valid_t, c_ref, m_ref, forward):
    L = qc.shape[0]
    d = vt.shape[0]
    a_row = i_row - b_row
    m_st = m_ref[...]
    log_d = jnp.where(valid_t, a_col + b_row, -jnp.inf)
    m_t = jnp.maximum(b_row + m_st, jnp.max(log_d, axis=0, keepdims=True))
    w_intra = lax.dot_general(kc, qc, _NT, preferred_element_type=F32) * jnp.exp(log_d - m_t)
    s_inter = jnp.exp(b_row + m_st - m_t)
    c_aug = c_ref[...]
    inter = lax.dot_general(c_aug.astype(BF16), qc, _NT, preferred_element_type=F32)
    vt_aug = jnp.concatenate([vt, jnp.ones((ONES_ROWS, L), BF16)], axis=0)
    intra = jnp.dot(vt_aug, w_intra.astype(BF16), preferred_element_type=F32)
    num = intra[:d] + s_inter * inter[:d]
    den = intra[d:d + 1] + s_inter * inter[d:d + 1]
    h_t = num / jnp.maximum(jnp.abs(den), jnp.exp(-m_t))

    b_last = b_row[:, L - 1:L] if forward else b_row[:, 0:1]
    m_new = jnp.maximum(b_last + m_st, jnp.max(b_last + a_row, axis=-1, keepdims=True))
    w_row = jnp.exp(b_last + a_row - m_new)
    decay = jnp.exp(b_last + m_st - m_new)
    w_b = w_row.astype(BF16)
    lhs = jnp.concatenate([vt * w_b, jnp.broadcast_to(w_b, (ONES_ROWS, L))], axis=0)
    c_ref[...] = decay * c_aug + jnp.dot(lhs, kc, preferred_element_type=F32)
    m_ref[...] = m_new
    return h_t.T


def _mlstm_body(qf_ref, kf_ref, vf_ref, gf_ref, gtf_ref, qb_ref, kb_ref, vb_ref, gb_ref, gtb_ref,
                brow_ref, bcol_ref, hf_ref, hb_ref, c_scr, m_scr, *, heads):
    @pl.when(pl.program_id(1) == 0)
    def _():
        for ref in (c_scr, m_scr):
            ref[...] = jnp.zeros(ref.shape, ref.dtype)

    L = qf_ref.shape[0]
    d = qf_ref.shape[1] // heads
    r = lax.broadcasted_iota(jnp.int32, (L, L), 0)
    c = lax.broadcasted_iota(jnp.int32, (L, L), 1)
    lower, upper = c <= r, c >= r
    tri = {True: lower.astype(BF16), False: upper.astype(BF16)}

    for forward, q_ref, k_ref, v_ref, g_ref, gt_ref, out_ref in (
            (True, qf_ref, kf_ref, vf_ref, gf_ref, gtf_ref, hf_ref),
            (False, qb_ref, kb_ref, vb_ref, gb_ref, gtb_ref, hb_ref)):
        gi = 0 if forward else 2
        g = g_ref[...] + brow_ref[...]
        gt = gt_ref[...] + bcol_ref[...]
        cum_col = sum(jnp.dot(tri[forward], part, preferred_element_type=F32)
                      for part in _split3(_log_sigmoid(g)))
        cum_row = sum(jnp.dot(part, tri[not forward], preferred_element_type=F32)
                      for part in _split3(_log_sigmoid(gt)))
        valid_t = upper if forward else lower
        for h in range(heads):
            ii, fi = h * 4 + gi, h * 4 + gi + 1
            cols = slice(h * d, (h + 1) * d)
            ci = h if forward else heads + h
            out_ref[:, cols] = _mlstm_chunk(
                q_ref[:, cols], k_ref[:, cols], v_ref[cols, :],
                gt[ii:ii + 1, :], cum_row[fi:fi + 1, :], g[:, ii:ii + 1] - cum_col[:, fi:fi + 1],
                valid_t, c_scr.at[ci], m_scr.at[ci], forward)


def _mlstm(qk, avt, gates, gates_t, bias_row, bias_col, *, batch, seq, heads, head_dim):
    n = qk.shape[0]
    L = ML_CHUNK
    nc = seq // L
    width = heads * head_dim
    n_gates = gates_t.shape[0]
    fw = lambda b, c: b * nc + c
    bw = lambda b, c: b * nc + (nc - 1 - c)

    def specs(rowf):
        return [
            pl.BlockSpec((L, width), lambda b, c: (rowf(b, c), 0)),
            pl.BlockSpec((L, width), lambda b, c: (rowf(b, c), 1)),
            pl.BlockSpec((width, L), lambda b, c: (0, rowf(b, c))),
            pl.BlockSpec((L, LANES), lambda b, c: (rowf(b, c), 0)),
            pl.BlockSpec((n_gates, L), lambda b, c: (0, rowf(b, c))),
        ]

    def call_args():
        return [qk, qk, avt, gates, gates_t]

    return pl.pallas_call(
        functools.partial(_mlstm_body, heads=heads),
        grid=(batch, nc),
        in_specs=specs(fw) + specs(bw) + [
            pl.BlockSpec((1, LANES), lambda b, c: (0, 0)),
            pl.BlockSpec((n_gates, 1), lambda b, c: (0, 0)),
        ],
        out_specs=[pl.BlockSpec((L, width), lambda b, c: (fw(b, c), 0)),
                   pl.BlockSpec((L, width), lambda b, c: (bw(b, c), 0))],
        out_shape=[jax.ShapeDtypeStruct((n, width), F32)] * 2,
        scratch_shapes=[
            pltpu.VMEM((2 * heads, head_dim + ONES_ROWS, head_dim), F32),
            pltpu.VMEM((2 * heads, 1, 1), F32),
        ],
        compiler_params=_params(2),
        name="mlstm",
    )(*call_args(), *call_args(), bias_row, bias_col)


def _attn_body(lam_ref, g_ref, q_ref, k_ref, vt_ref, bz_ref, o_ref, m_scr, acc_scr,
               s_scr0, s_scr1, p_scr0, p_scr1, a_scr0, a_scr1, x_scr0, x_scr1, *, lambda_init):
    tq = q_ref.shape[0]
    n_chunks, d_v, tk = vt_ref.shape
    q = q_ref[...]
    lane = lax.broadcasted_iota(jnp.int32, (tq, LANES), 1)
    zero = jnp.zeros_like(q)
    q_maps = (jnp.where(lane < LANES // 2, q, zero), jnp.where(lane >= LANES // 2, q, zero))
    s_scr, p_scr, a_scr = (s_scr0, s_scr1), (p_scr0, p_scr1), (a_scr0, a_scr1)
    x_scr = (x_scr0, x_scr1)
    ones_rows = jnp.ones((acc_scr.shape[1] - d_v, tk), BF16)

    m_scr[...] = jnp.full(m_scr.shape, -jnp.inf, F32)
    acc_scr[...] = jnp.zeros(acc_scr.shape, F32)

    def scores(j, slot):
        k = k_ref[pl.ds(pl.multiple_of(j * tk, tk), tk), :]
        for mi in range(2):
            s = lax.dot_general(k, q_maps[mi], _NT, preferred_element_type=F32)
            s_scr[slot][mi, :, :tq] = s
            x_scr[slot][mi] = jnp.max(s, axis=0, keepdims=True)

    def softmax(slot):
        for mi in range(2):
            m_prev = m_scr[mi]
            m_cur = jnp.maximum(m_prev, x_scr[slot][mi])
            m_scr[mi] = m_cur
            a_scr[slot][mi] = jnp.exp2(m_prev - m_cur)
            p_scr[slot][mi, :, :tq] = jnp.exp2(s_scr[slot][mi, :, :tq] - m_cur).astype(BF16)

    def values(j, slot):
        vt = jnp.concatenate([vt_ref[j], ones_rows], axis=0)
        for mi in range(2):
            acc_scr[mi] = a_scr[slot][mi] * acc_scr[mi] + jnp.dot(
                vt, p_scr[slot][mi, :, :tq], preferred_element_type=F32)

    scores(0, 0)
    scores(1, 1)
    softmax(0)

    def two_steps(t):
        j = 2 * t + 1
        scores(j + 1, 0)
        softmax(1)
        values(j - 1, 0)
        scores(j + 2, 1)
        softmax(0)
        values(j, 1)

    n_pairs = (n_chunks - 2) // 2
    unroll = min(ATTN_UNROLL_PAIRS, max(n_pairs, 1))
    head = n_pairs % unroll
    for t in range(head):
        two_steps(t)

    def loop_body(i, carry):
        for u in range(unroll):
            two_steps(head + i * unroll + u)
        return carry

    lax.fori_loop(0, n_pairs // unroll, loop_body, 0)
    softmax(1)
    values(n_chunks - 2, 0)
    values(n_chunks - 1, 1)

    lp = lam_ref[...]
    dots = jnp.sum(lp[0:1, :] * lp[1:2, :], axis=-1, keepdims=True), \
        jnp.sum(lp[2:3, :] * lp[3:4, :], axis=-1, keepdims=True)
    lam = jnp.exp(dots[0]) - jnp.exp(dots[1]) + lambda_init
    acc0, acc1 = acc_scr[0], acc_scr[1]
    att = (acc0[:d_v] / acc0[d_v:d_v + 1] - lam * (acc1[:d_v] / acc1[d_v:d_v + 1]))
    ms = jnp.mean(att * att, axis=0, keepdims=True)
    att = (att * lax.rsqrt(ms + NORM_EPS)).T
    att = att * g_ref[...] * (1.0 - lambda_init)
    o_ref[...] = (att * _silu(bz_ref[...].astype(F32))).astype(BF16)


def _diff_attn(lam_p, subln_g, vt, p, *, batch, seq, heads, q_col, k_col, z_col, tq, lambda_init):
    n = p.shape[0]
    nq = seq // tq
    tk = vt.shape[2]
    nk = seq // tk
    body = functools.partial(_attn_body, lambda_init=lambda_init)
    return pl.pallas_call(
        body,
        grid=(batch, heads, nq),
        in_specs=[
            pl.BlockSpec(lam_p.shape, lambda b, h, i: (0, 0)),
            pl.BlockSpec((1, LANES), lambda b, h, i: (0, 0)),
            pl.BlockSpec((tq, LANES), lambda b, h, i: (b * nq + i, q_col + h)),
            pl.BlockSpec((seq, LANES), lambda b, h, i: (b, k_col + h)),
            pl.BlockSpec((nk, LANES, tk), lambda b, h, i: (b, h, 0)),
            pl.BlockSpec((tq, LANES), lambda b, h, i: (b * nq + i, z_col + h)),
        ],
        out_specs=pl.BlockSpec((tq, LANES), lambda b, h, i: (b * nq + i, h)),
        out_shape=jax.ShapeDtypeStruct((n, heads * LANES), BF16),
        scratch_shapes=[
            pltpu.VMEM((2, 1, tq), F32),
            pltpu.VMEM((2, LANES + ONES_ROWS, tq), F32),
            pltpu.VMEM((2, tk, tq + LANES), F32), pltpu.VMEM((2, tk, tq + LANES), F32),
            pltpu.VMEM((2, tk, tq + LANES), BF16), pltpu.VMEM((2, tk, tq + LANES), BF16),
            pltpu.VMEM((2, 1, tq), F32), pltpu.VMEM((2, 1, tq), F32),
            pltpu.VMEM((2, 1, tq), F32), pltpu.VMEM((2, 1, tq), F32),
        ],
        compiler_params=_params(3),
        name="diff_attn",
    )(lam_p, subln_g, p, p, vt, p)


def _merge_body(hf_ref, hb_ref, ao_ref, az_ref, yb_ref, ga_ref, gb_ref, x_ref, mlg_ref, gbias_ref,
                fing_ref, wa_ref, wb_ref, wo_ref, o_ref, *, head_dim, final_norm):
    hs = hf_ref[...] + hb_ref[...]
    width = hs.shape[1]
    parts = []
    for c0 in range(0, width, head_dim):
        hh = hs[:, c0:c0 + head_dim]
        ms = jnp.mean(hh * hh, axis=-1, keepdims=True)
        parts.append(hh * lax.rsqrt(ms + NORM_EPS))
    hm = jnp.concatenate(parts, axis=-1) * mlg_ref[...]
    y_a = hm * _sigmoid(ao_ref[...].astype(F32)) * _silu(az_ref[...].astype(F32))
    pa = jnp.dot(y_a.astype(BF16), wa_ref[...], preferred_element_type=F32)
    pb = jnp.dot(yb_ref[...], wb_ref[...], preferred_element_type=F32)
    gbias = gbias_ref[...]
    mix = (_sigmoid(ga_ref[...].astype(F32) + gbias[0:1, :]) * pa
           + _sigmoid(gb_ref[...].astype(F32) + gbias[1:2, :]) * pb)
    xn = x_ref[...] + jnp.dot(mix.astype(BF16), wo_ref[...], preferred_element_type=F32)
    if final_norm:
        ms = jnp.mean(xn * xn, axis=-1, keepdims=True)
        xn = xn * lax.rsqrt(ms + NORM_EPS) * fing_ref[...]
    o_ref[...] = xn


def _merge(h_fw, h_bw, p, y_b, x2, ml_g, gate_bias, final_g, w_a, w_b, w_o, *, tm, head_dim,
           o_col, z_col, ga_col, gb_col, final_norm):
    n, d = x2.shape
    row = lambda c: pl.BlockSpec((tm, d), lambda i: (i, c))
    const = lambda shape: pl.BlockSpec(shape, lambda i: (0, 0))
    body = functools.partial(_merge_body, head_dim=head_dim, final_norm=final_norm)
    return pl.pallas_call(
        body,
        grid=(n // tm,),
        in_specs=[row(0), row(0), row(o_col), row(z_col), row(0), row(ga_col), row(gb_col), row(0),
                  const((1, d)), const((2, d)), const((1, d)),
                  const((d, d)), const((d, d)), const((d, d))],
        out_specs=row(0),
        out_shape=jax.ShapeDtypeStruct((n, d), F32),
        compiler_params=_params(1),
        name="merge",
    )(h_fw, h_bw, p, p, y_b, p, p, x2, ml_g, gate_bias, final_g, w_a, w_b, w_o)


def kernel(x, positions, norm_g, w_in, ml_gate_b, ml_conv_w, ml_norm_g, da_lambda, da_subln_g,
           gate_b, w_branch_a, w_branch_b, w_out, final_g):
    batch, seq, d = x.shape
    depth = w_in.shape[0]
    ml_heads, ml_head_dim = ml_norm_g.shape[1], ml_norm_g.shape[2]
    ml_width = ml_heads * ml_head_dim
    n_gates = 4 * ml_heads
    da_v_dim = da_subln_g.shape[1]
    da_qk_dim = da_lambda.shape[2]
    da_width = w_branch_b.shape[1]
    da_heads = da_width // da_v_dim
    assert ml_width == d and da_width == d and da_v_dim == LANES and 2 * da_qk_dim == LANES
    assert w_in.shape[2] == 11 * d + n_gates and seq % ML_CHUNK == 0
    n = batch * seq

    col = {name: i for i, name in enumerate(
        ("a_q", "a_k", "a_o", "a_z", "b_q", "b_k", "b_z", "g_a", "g_b"))}
    bv0 = 7 * d + n_gates

    tm_proj = min(1024, seq)
    tm_prep = min(512, seq)
    tq = min(512, seq)
    tk = min(512, seq)
    tm_merge = min(256, seq)

    x2 = x.reshape(n, d)
    pos2 = positions.reshape(n, 1)
    inv_freq = ROPE_THETA ** (-jnp.arange(0, da_qk_dim, 2, dtype=F32) / da_qk_dim)
    freq_tile = jnp.tile(inv_freq, LANES // inv_freq.shape[0]).reshape(1, LANES)

    for layer in range(depth):
        w = w_in[layer]
        w_main = jnp.concatenate([w[:, :2 * d], w[:, 3 * d:5 * d], w[:, 5 * d + n_gates:bv0],
                                  w[:, bv0 + d:]], axis=1).astype(BF16)
        wvt = w[:, bv0:bv0 + d].T.astype(BF16)
        wavt = w[:, 2 * d:3 * d].T.astype(BF16)
        w_g = w[:, 5 * d:5 * d + n_gates].reshape(d, 4, ml_heads).transpose(0, 2, 1)
        w_g = w_g.reshape(d, n_gates).astype(BF16)
        wg = jnp.pad(w_g, ((0, 0), (0, LANES - n_gates)))
        bias_hg = ml_gate_b[layer].astype(F32).T
        bias_row = jnp.pad(bias_hg.reshape(1, n_gates), ((0, 0), (0, LANES - n_gates)))
        bias_col = bias_hg.reshape(n_gates, 1)

        p, gates, gates_t, vt, avt = _in_proj(
            x2, norm_g[layer].reshape(1, d), w_main, wg, w_g.T, wvt, wavt, pos2, freq_tile,
            tm=tm_proj, tn=d, tk=tk, rope_cols=(col["b_q"], col["b_k"]),
            q_scale=da_qk_dim ** -0.5 * math.log2(math.e))
        qk = _prep(p, ml_conv_w[layer].astype(F32), seq=seq, ml_width=ml_width, tm=tm_prep,
                   k_scale=ml_head_dim ** -0.5)
        h_fw, h_bw = _mlstm(qk, avt, gates, gates_t, bias_row, bias_col,
                            batch=batch, seq=seq, heads=ml_heads, head_dim=ml_head_dim)
        lambda_init = 0.8 - 0.6 * math.exp(-0.3 * layer)
        y_b = _diff_attn(da_lambda[layer].astype(F32), da_subln_g[layer].reshape(1, LANES), vt, p,
                         batch=batch, seq=seq, heads=da_heads, q_col=col["b_q"] * da_heads,
                         k_col=col["b_k"] * da_heads, z_col=col["b_z"] * da_heads, tq=tq,
                         lambda_init=lambda_init)
        x2 = _merge(h_fw, h_bw, p, y_b, x2, ml_norm_g[layer].reshape(1, ml_width), gate_b[layer],
                    final_g.reshape(1, d), w_branch_a[layer].astype(BF16),
                    w_branch_b[layer].astype(BF16), w_out[layer].astype(BF16), tm=tm_merge,
                    head_dim=ml_head_dim, o_col=col["a_o"], z_col=col["a_z"], ga_col=col["g_a"],
                    gb_col=col["g_b"], final_norm=(layer == depth - 1))
    return x2.reshape(batch, seq, d)
```

```python
import functools
import math

import jax
import jax.numpy as jnp
from jax import lax
from jax.experimental import pallas as pl
from jax.experimental.pallas import tpu as pltpu

F32 = jnp.float32
BF16 = jnp.bfloat16

NORM_EPS = 1e-6
ROPE_THETA = 10000.0
GATES_PER_HEAD = 4
LANES = 128
SUBLANES = 8
ONES_ROWS = 16
MXU_WIDTH = 256
V7X_VMEM_BYTES = 64 * 1024 * 1024
VMEM_LIMIT_BYTES = V7X_VMEM_BYTES - 8 * 1024 * 1024

ML_CHUNK = MXU_WIDTH
CONV_ROWS = 128
CONV_SLAB_LANES = 256
ATTN_UNROLL_PAIRS = 3

_NT = (((1,), (1,)), ((), ()))


def _tiles(seq):
    t = dict(in_proj=min(1024, seq), prep=min(512, seq), attn_q=min(512, seq),
             attn_k=min(512, seq), merge=min(256, seq))
    assert all(seq % v == 0 for v in t.values()) and seq % ML_CHUNK == 0
    assert (seq // t["attn_k"]) % 2 == 0 and t["in_proj"] % t["attn_k"] == 0
    return t


def _params(n_axes):
    return pltpu.CompilerParams(
        dimension_semantics=("arbitrary",) * n_axes,
        vmem_limit_bytes=VMEM_LIMIT_BYTES)


def _sigmoid(x):
    return 1.0 / (1.0 + jnp.exp2(x * (-math.log2(math.e))))


def _silu(x):
    return x * _sigmoid(x)


def _log_sigmoid(x):
    return jnp.minimum(x, 0.0) - jnp.log1p(jnp.exp(-jnp.abs(x)))


def _rope_tables(pos_ref, freq_ref):
    tm = pos_ref.shape[0]
    d_map = LANES // 2
    n_freq = d_map // 2
    groups = LANES // n_freq
    rows_g = tm // groups
    lane_c = lax.broadcasted_iota(jnp.int32, (rows_g, LANES), 1)
    pos_c = pos_ref[0:rows_g, :]
    for g in range(1, groups):
        pos_c = jnp.where(lane_c >= g * n_freq, pos_ref[g * rows_g:(g + 1) * rows_g, :], pos_c)
    ang_c = pos_c.astype(F32) * freq_ref[...]

    def spread(table_c):
        parts = []
        for g in range(groups):
            base = table_c if g == 0 else pltpu.roll(table_c, LANES - g * n_freq, 1)
            full = base
            for r in range(1, groups):
                full = jnp.where(lane_c >= r * n_freq, pltpu.roll(base, r * n_freq, 1), full)
            parts.append(full)
        return jnp.concatenate(parts, axis=0)

    lane = lax.broadcasted_iota(jnp.int32, (tm, LANES), 1)
    sin = spread(jnp.sin(ang_c))
    return spread(jnp.cos(ang_c)), jnp.where(lane < LANES // 2, -sin, sin)


def _in_proj_body(x_ref, g_ref, w_ref, wg_ref, wgt_ref, wvt_ref, wavt_ref, pos_ref, freq_ref,
                  p_ref, gates_ref, gatest_ref, vt_ref, avt_ref, h_scr, cos_scr, sin_scr,
                  *, rope_cols, q_scale):
    j = pl.program_id(1)

    @pl.when(j == 0)
    def _():
        cos_scr[...], sin_scr[...] = _rope_tables(pos_ref, freq_ref)
        x = x_ref[...]
        ms = jnp.mean(x * x, axis=-1, keepdims=True)
        h = (x * lax.rsqrt(ms + NORM_EPS) * g_ref[...]).astype(BF16)
        h_scr[...] = h
        gates_ref[...] = jnp.dot(h, wg_ref[...], preferred_element_type=F32)
        gatest_ref[...] = lax.dot_general(wgt_ref[...], h, _NT, preferred_element_type=F32)
        avt_ref[...] = lax.dot_general(wavt_ref[...], h, _NT,
                                       preferred_element_type=F32).astype(BF16)
        vt = lax.dot_general(wvt_ref[...], h, _NT, preferred_element_type=F32).astype(BF16)
        tk = vt_ref.shape[2]
        for c in range(vt_ref.shape[0]):
            vt_ref[c] = vt[:, c * tk:(c + 1) * tk]

    is_rope = functools.reduce(jnp.logical_or, [j == c for c in rope_cols])

    @pl.when(jnp.logical_not(is_rope))
    def _():
        p_ref[...] = jnp.dot(h_scr[...], w_ref[...], preferred_element_type=F32).astype(BF16)

    @pl.when(is_rope)
    def _():
        r = jnp.dot(h_scr[...], w_ref[...], preferred_element_type=F32)
        scale = jnp.where(j == rope_cols[0], q_scale, 1.0)
        cos = cos_scr[...] * scale
        sin = sin_scr[...] * scale
        for c0 in range(0, r.shape[1], LANES):
            t = r[:, c0:c0 + LANES]
            p_ref[:, c0:c0 + LANES] = (t * cos + pltpu.roll(t, LANES // 2, 1) * sin).astype(BF16)


def _in_proj(x2, norm_g, w_main, wg, wgt, wvt, wavt, pos2, freq_tile, *, tm, tn, tk, rope_cols,
             q_scale):
    n, d = x2.shape
    width = w_main.shape[1]
    ng = wgt.shape[0]
    dv = wvt.shape[0]
    dav = wavt.shape[0]
    return pl.pallas_call(
        functools.partial(_in_proj_body, rope_cols=rope_cols, q_scale=q_scale),
        grid=(n // tm, width // tn),
        in_specs=[
            pl.BlockSpec((tm, d), lambda i, j: (i, 0)),
            pl.BlockSpec((1, d), lambda i, j: (0, 0)),
            pl.BlockSpec((d, tn), lambda i, j: (0, j)),
            pl.BlockSpec((d, LANES), lambda i, j: (0, 0)),
            pl.BlockSpec((ng, d), lambda i, j: (0, 0)),
            pl.BlockSpec((dv, d), lambda i, j: (0, 0)),
            pl.BlockSpec((dav, d), lambda i, j: (0, 0)),
            pl.BlockSpec((tm, 1), lambda i, j: (i, 0)),
            pl.BlockSpec((1, LANES), lambda i, j: (0, 0)),
        ],
        out_specs=[
            pl.BlockSpec((tm, tn), lambda i, j: (i, j)),
            pl.BlockSpec((tm, LANES), lambda i, j: (i, 0)),
            pl.BlockSpec((ng, tm), lambda i, j: (0, i)),
            pl.BlockSpec((tm // tk, dv, tk), lambda i, j: (i, 0, 0)),
            pl.BlockSpec((dav, tm), lambda i, j: (0, i)),
        ],
        out_shape=[
            jax.ShapeDtypeStruct((n, width), BF16),
            jax.ShapeDtypeStruct((n, LANES), F32),
            jax.ShapeDtypeStruct((ng, n), F32),
            jax.ShapeDtypeStruct((n // tk, dv, tk), BF16),
            jax.ShapeDtypeStruct((dav, n), BF16),
        ],
        scratch_shapes=[pltpu.VMEM((tm, d), BF16), pltpu.VMEM((tm, LANES), F32),
                        pltpu.VMEM((tm, LANES), F32)],
        compiler_params=_params(2),
        name="in_proj",
    )(x2, norm_g, w_main, wg, wgt, wvt, wavt, pos2, freq_tile)


def _conv_shift_matrix(conv_width):
    halo, pad, win = SUBLANES, conv_width // 2, CONV_ROWS + 2 * SUBLANES
    k_pad = -(-conv_width * win // MXU_WIDTH) * MXU_WIDTH
    r = jnp.arange(CONV_ROWS)[:, None]
    c = jnp.arange(k_pad)[None, :]
    hit = functools.reduce(jnp.logical_or,
                           [c == t * win + r + t + halo - pad for t in range(conv_width)])
    return hit.astype(BF16)


def _prep_body(qk_ref, prev_ref, next_ref, cw_ref, shift_ref, qk_out, ext_scr, *, tiles_per_seq,
               conv_width, k_scale):
    i = pl.program_id(0)
    tm = qk_ref.shape[0]
    width = qk_ref.shape[1]
    halo = SUBLANES
    win = CONV_ROWS + 2 * halo
    t_in_seq = i % tiles_per_seq
    has_prev = (t_in_seq > 0).astype(F32)
    has_next = (t_in_seq < tiles_per_seq - 1).astype(F32)
    shift = shift_ref[...]
    slab = ext_scr.shape[1]
    k_fill = jnp.zeros((shift.shape[1] - conv_width * win, slab), BF16)

    for c0 in range(0, width, slab):
        cols = slice(c0, c0 + slab)
        ext_scr[0:halo, :] = prev_ref[:, cols].astype(F32) * has_prev
        ext_scr[halo:halo + tm, :] = qk_ref[:, cols].astype(F32)
        ext_scr[halo + tm:halo + tm + halo, :] = next_ref[:, cols].astype(F32) * has_next
        scale = 1.0 if c0 < width // 2 else k_scale
        taps = [cw_ref[t:t + 1, cols] for t in range(conv_width)]
        for r0 in range(0, tm, CONV_ROWS):
            window = ext_scr[r0:r0 + win, :]
            prods = jnp.concatenate([(window * tap).astype(BF16) for tap in taps] + [k_fill], axis=0)
            acc = jnp.dot(shift, prods, preferred_element_type=F32)
            qk_out[r0:r0 + CONV_ROWS, cols] = (_silu(acc) * scale).astype(BF16)


def _prep(p, conv_w, *, seq, ml_width, tm, k_scale):
    n = p.shape[0]
    n_tiles = n // tm
    halo_blocks = n // SUBLANES
    qk_w = 2 * ml_width
    assert conv_w.shape[0] // 2 <= SUBLANES
    shift = _conv_shift_matrix(conv_w.shape[0])
    body = functools.partial(
        _prep_body, tiles_per_seq=seq // tm, conv_width=conv_w.shape[0], k_scale=k_scale)
    per_tile = tm // SUBLANES
    return pl.pallas_call(
        body,
        grid=(n_tiles,),
        in_specs=[
            pl.BlockSpec((tm, qk_w), lambda i: (i, 0)),
            pl.BlockSpec((SUBLANES, qk_w), lambda i: (jnp.maximum(i * per_tile - 1, 0), 0)),
            pl.BlockSpec((SUBLANES, qk_w),
                         lambda i: (jnp.minimum((i + 1) * per_tile, halo_blocks - 1), 0)),
            pl.BlockSpec((conv_w.shape[0], qk_w), lambda i: (0, 0)),
            pl.BlockSpec(shift.shape, lambda i: (0, 0)),
        ],
        out_specs=pl.BlockSpec((tm, qk_w), lambda i: (i, 0)),
        out_shape=jax.ShapeDtypeStruct((n, qk_w), BF16),
        scratch_shapes=[pltpu.VMEM((tm + 2 * SUBLANES, CONV_SLAB_LANES), F32)],
        compiler_params=_params(1),
        name="prep",
    )(p, p, p, conv_w, shift)


def _split3(f):
    hi = f.astype(BF16)
    r1 = f - hi.astype(F32)
    mid = r1.astype(BF16)
    lo = (r1 - mid.astype(F32)).astype(BF16)
    return hi, mid, lo


def _mlstm_chunk(qc, kc, vt, i_row, b_row, a_col, valid_t, c_ref, m_ref, forward):
    L = qc.shape[0]
    d = vt.shape[0]
    a_row = i_row - b_row
    m_st = m_ref[...]
    log_d = jnp.where(valid_t, a_col + b_row, -jnp.inf)
    m_t = jnp.maximum(b_row + m_st, jnp.max(log_d, axis=0, keepdims=True))
    w_intra = lax.dot_general(kc, qc, _NT, preferred_element_type=F32) * jnp.exp(log_d - m_t)
    s_inter = jnp.exp(b_row + m_st - m_t)
    c_aug = c_ref[...]
    inter = lax.dot_general(c_aug.astype(BF16), qc, _NT, preferred_element_type=F32)
    vt_aug = jnp.concatenate([vt, jnp.ones((ONES_ROWS, L), BF16)], axis=0)
    intra = jnp.dot(vt_aug, w_intra.astype(BF16), preferred_element_type=F32)
    num = intra[:d] + s_inter * inter[:d]
    den = intra[d:d + 1] + s_inter * inter[d:d + 1]
    h_t = num / jnp.maximum(jnp.abs(den), jnp.exp(-m_t))

    b_last = b_row[:, L - 1:L] if forward else b_row[:, 0:1]
    m_new = jnp.maximum(b_last + m_st, jnp.max(b_last + a_row, axis=-1, keepdims=True))
    w_row = jnp.exp(b_last + a_row - m_new)
    decay = jnp.exp(b_last + m_st - m_new)
    w_b = w_row.astype(BF16)
    lhs = jnp.concatenate([vt * w_b, jnp.broadcast_to(w_b, (ONES_ROWS, L))], axis=0)
    c_ref[...] = decay * c_aug + jnp.dot(lhs, kc, preferred_element_type=F32)
    m_ref[...] = m_new
    return h_t.T


def _mlstm_body(qf_ref, kf_ref, vf_ref, gf_ref, gtf_ref, qb_ref, kb_ref, vb_ref, gb_ref, gtb_ref,
                brow_ref, bcol_ref, hf_ref, hb_ref, c_scr, m_scr, *, heads):
    @pl.when(pl.program_id(1) == 0)
    def _():
        for ref in (c_scr, m_scr):
            ref[...] = jnp.zeros(ref.shape, ref.dtype)

    L = qf_ref.shape[0]
    d = qf_ref.shape[1] // heads
    r = lax.broadcasted_iota(jnp.int32, (L, L), 0)
    c = lax.broadcasted_iota(jnp.int32, (L, L), 1)
    lower, upper = c <= r, c >= r
    tri = {True: lower.astype(BF16), False: upper.astype(BF16)}

    for forward, q_ref, k_ref, v_ref, g_ref, gt_ref, out_ref in (
            (True, qf_ref, kf_ref, vf_ref, gf_ref, gtf_ref, hf_ref),
            (False, qb_ref, kb_ref, vb_ref, gb_ref, gtb_ref, hb_ref)):
        gi = 0 if forward else 2
        g = g_ref[...] + brow_ref[...]
        gt = gt_ref[...] + bcol_ref[...]
        cum_col = sum(jnp.dot(tri[forward], part, preferred_element_type=F32)
                      for part in _split3(_log_sigmoid(g)))
        cum_row = sum(jnp.dot(part, tri[not forward], preferred_element_type=F32)
                      for part in _split3(_log_sigmoid(gt)))
        valid_t = upper if forward else lower
        for h in range(heads):
            ii, fi = h * GATES_PER_HEAD + gi, h * GATES_PER_HEAD + gi + 1
            cols = slice(h * d, (h + 1) * d)
            ci = h if forward else heads + h
            out_ref[:, cols] = _mlstm_chunk(
                q_ref[:, cols], k_ref[:, cols], v_ref[cols, :],
                gt[ii:ii + 1, :], cum_row[fi:fi + 1, :], g[:, ii:ii + 1] - cum_col[:, fi:fi + 1],
                valid_t, c_scr.at[ci], m_scr.at[ci], forward)


def _mlstm(qk, avt, gates, gates_t, bias_row, bias_col, *, batch, seq, heads, head_dim):
    n = qk.shape[0]
    L = ML_CHUNK
    nc = seq // L
    width = heads * head_dim
    n_gates = gates_t.shape[0]
    fw = lambda b, c: b * nc + c
    bw = lambda b, c: b * nc + (nc - 1 - c)

    def specs(rowf):
        return [
            pl.BlockSpec((L, width), lambda b, c: (rowf(b, c), 0)),
            pl.BlockSpec((L, width), lambda b, c: (rowf(b, c), 1)),
            pl.BlockSpec((width, L), lambda b, c: (0, rowf(b, c))),
            pl.BlockSpec((L, LANES), lambda b, c: (rowf(b, c), 0)),
            pl.BlockSpec((n_gates, L), lambda b, c: (0, rowf(b, c))),
        ]

    def call_args():
        return [qk, qk, avt, gates, gates_t]

    return pl.pallas_call(
        functools.partial(_mlstm_body, heads=heads),
        grid=(batch, nc),
        in_specs=specs(fw) + specs(bw) + [
            pl.BlockSpec((1, LANES), lambda b, c: (0, 0)),
            pl.BlockSpec((n_gates, 1), lambda b, c: (0, 0)),
        ],
        out_specs=[pl.BlockSpec((L, width), lambda b, c: (fw(b, c), 0)),
                   pl.BlockSpec((L, width), lambda b, c: (bw(b, c), 0))],
        out_shape=[jax.ShapeDtypeStruct((n, width), F32)] * 2,
        scratch_shapes=[
            pltpu.VMEM((2 * heads, head_dim + ONES_ROWS, head_dim), F32),
            pltpu.VMEM((2 * heads, 1, 1), F32),
        ],
        compiler_params=_params(2),
        name="mlstm",
    )(*call_args(), *call_args(), bias_row, bias_col)


def _attn_body(lam_ref, g_ref, q_ref, k_ref, vt_ref, bz_ref, o_ref, m_scr, acc_scr,
               s_scr0, s_scr1, p_scr0, p_scr1, a_scr0, a_scr1, x_scr0, x_scr1, *, lambda_init):
    tq = q_ref.shape[0]
    n_chunks, d_v, tk = vt_ref.shape
    q = q_ref[...]
    lane = lax.broadcasted_iota(jnp.int32, (tq, LANES), 1)
    zero = jnp.zeros_like(q)
    first = (lane % (LANES // 2)) < LANES // 4
    q_maps = (jnp.where(first, q, zero), jnp.where(first, zero, q))
    s_scr, p_scr, a_scr = (s_scr0, s_scr1), (p_scr0, p_scr1), (a_scr0, a_scr1)
    x_scr = (x_scr0, x_scr1)
    ones_rows = jnp.ones((acc_scr.shape[1] - d_v, tk), BF16)

    m_scr[...] = jnp.full(m_scr.shape, -jnp.inf, F32)
    acc_scr[...] = jnp.zeros(acc_scr.shape, F32)

    def scores(j, slot):
        k = k_ref[pl.ds(pl.multiple_of(j * tk, tk), tk), :]
        for mi in range(2):
            s = lax.dot_general(k, q_maps[mi], _NT, preferred_element_type=F32)
            s_scr[slot][mi] = s
            x_scr[slot][mi] = jnp.max(s, axis=0, keepdims=True)

    def softmax(slot):
        for mi in range(2):
            m_prev = m_scr[mi]
            m_cur = jnp.maximum(m_prev, x_scr[slot][mi])
            m_scr[mi] = m_cur
            a_scr[slot][mi] = jnp.exp2(m_prev - m_cur)
            p_scr[slot][mi] = jnp.exp2(s_scr[slot][mi] - m_cur).astype(BF16)

    def values(j, slot):
        vt = jnp.concatenate([vt_ref[j], ones_rows], axis=0)
        for mi in range(2):
            acc_scr[mi] = a_scr[slot][mi] * acc_scr[mi] + jnp.dot(
                vt, p_scr[slot][mi], preferred_element_type=F32)

    def two_steps(t):
        j = 2 * t + 1
        scores(j + 1, 0)
        softmax(1)
        values(j - 1, 0)
        scores(j + 2, 1)
        softmax(0)
        values(j, 1)

    scores(0, 0)
    scores(1, 1)
    softmax(0)
    n_pairs = (n_chunks - 2) // 2
    unroll = min(ATTN_UNROLL_PAIRS, max(n_pairs, 1))
    head = n_pairs % unroll
    for t in range(head):
        two_steps(t)

    def loop_body(i, carry):
        for u in range(unroll):
            two_steps(head + i * unroll + u)
        return carry

    lax.fori_loop(0, n_pairs // unroll, loop_body, 0)
    softmax(1)
    values(n_chunks - 2, 0)
    values(n_chunks - 1, 1)

    lp = lam_ref[...]
    dots = jnp.sum(lp[0:1, :] * lp[1:2, :], axis=-1, keepdims=True), \
        jnp.sum(lp[2:3, :] * lp[3:4, :], axis=-1, keepdims=True)
    lam = jnp.exp(dots[0]) - jnp.exp(dots[1]) + lambda_init
    acc0, acc1 = acc_scr[0], acc_scr[1]
    att = (acc0[:d_v] / acc0[d_v:d_v + 1] - lam * (acc1[:d_v] / acc1[d_v:d_v + 1]))
    ms = jnp.mean(att * att, axis=0, keepdims=True)
    att = (att * lax.rsqrt(ms + NORM_EPS)).T
    att = att * g_ref[...] * (1.0 - lambda_init)
    o_ref[...] = (att * _silu(bz_ref[...].astype(F32))).astype(BF16)


def _diff_attn(lam_p, subln_g, vt, p, *, batch, seq, heads, q_col, k_col, z_col, tq, lambda_init):
    n = p.shape[0]
    nq = seq // tq
    tk = vt.shape[2]
    nk = seq // tk
    body = functools.partial(_attn_body, lambda_init=lambda_init)
    return pl.pallas_call(
        body,
        grid=(batch, heads, nq),
        in_specs=[
            pl.BlockSpec(lam_p.shape, lambda b, h, i: (0, 0)),
            pl.BlockSpec((1, LANES), lambda b, h, i: (0, 0)),
            pl.BlockSpec((tq, LANES), lambda b, h, i: (b * nq + i, q_col + h)),
            pl.BlockSpec((seq, LANES), lambda b, h, i: (b, k_col + h)),
            pl.BlockSpec((nk, LANES, tk), lambda b, h, i: (b, h, 0)),
            pl.BlockSpec((tq, LANES), lambda b, h, i: (b * nq + i, z_col + h)),
        ],
        out_specs=pl.BlockSpec((tq, LANES), lambda b, h, i: (b * nq + i, h)),
        out_shape=jax.ShapeDtypeStruct((n, heads * LANES), BF16),
        scratch_shapes=[
            pltpu.VMEM((2, 1, tq), F32),
            pltpu.VMEM((2, LANES + ONES_ROWS, tq), F32),
            pltpu.VMEM((2, tk, tq), F32), pltpu.VMEM((2, tk, tq), F32),
            pltpu.VMEM((2, tk, tq), BF16), pltpu.VMEM((2, tk, tq), BF16),
            pltpu.VMEM((2, 1, tq), F32), pltpu.VMEM((2, 1, tq), F32),
            pltpu.VMEM((2, 1, tq), F32), pltpu.VMEM((2, 1, tq), F32),
        ],
        compiler_params=_params(3),
        name="diff_attn",
    )(lam_p, subln_g, p, p, vt, p)


def _merge_body(hf_ref, hb_ref, ao_ref, az_ref, yb_ref, ga_ref, gb_ref, x_ref, mlg_ref, gbias_ref,
                fing_ref, wa_ref, wb_ref, wo_ref, o_ref, *, head_dim, final_norm):
    hs = hf_ref[...] + hb_ref[...]
    width = hs.shape[1]
    parts = []
    for c0 in range(0, width, head_dim):
        hh = hs[:, c0:c0 + head_dim]
        ms = jnp.mean(hh * hh, axis=-1, keepdims=True)
        parts.append(hh * lax.rsqrt(ms + NORM_EPS))
    hm = jnp.concatenate(parts, axis=-1) * mlg_ref[...]
    y_a = hm * _sigmoid(ao_ref[...].astype(F32)) * _silu(az_ref[...].astype(F32))
    pa = jnp.dot(y_a.astype(BF16), wa_ref[...], preferred_element_type=F32)
    pb = jnp.dot(yb_ref[...], wb_ref[...], preferred_element_type=F32)
    gbias = gbias_ref[...]
    mix = (_sigmoid(ga_ref[...].astype(F32) + gbias[0:1, :]) * pa
           + _sigmoid(gb_ref[...].astype(F32) + gbias[1:2, :]) * pb)
    xn = x_ref[...] + jnp.dot(mix.astype(BF16), wo_ref[...], preferred_element_type=F32)
    if final_norm:
        ms = jnp.mean(xn * xn, axis=-1, keepdims=True)
        xn = xn * lax.rsqrt(ms + NORM_EPS) * fing_ref[...]
    o_ref[...] = xn


def _merge(h_fw, h_bw, p, y_b, x2, ml_g, gate_bias, final_g, w_a, w_b, w_o, *, tm, head_dim,
           o_col, z_col, ga_col, gb_col, final_norm):
    n, d = x2.shape
    row = lambda c: pl.BlockSpec((tm, d), lambda i: (i, c))
    const = lambda shape: pl.BlockSpec(shape, lambda i: (0, 0))
    body = functools.partial(_merge_body, head_dim=head_dim, final_norm=final_norm)
    return pl.pallas_call(
        body,
        grid=(n // tm,),
        in_specs=[row(0), row(0), row(o_col), row(z_col), row(0), row(ga_col), row(gb_col), row(0),
                  const((1, d)), const((2, d)), const((1, d)),
                  const((d, d)), const((d, d)), const((d, d))],
        out_specs=row(0),
        out_shape=jax.ShapeDtypeStruct((n, d), F32),
        compiler_params=_params(1),
        name="merge",
    )(h_fw, h_bw, p, p, y_b, p, p, x2, ml_g, gate_bias, final_g, w_a, w_b, w_o)


def kernel(x, positions, norm_g, w_in, ml_gate_b, ml_conv_w, ml_norm_g, da_lambda, da_subln_g,
           gate_b, w_branch_a, w_branch_b, w_out, final_g):
    batch, seq, d = x.shape
    depth = w_in.shape[0]
    ml_heads, ml_head_dim = ml_norm_g.shape[1], ml_norm_g.shape[2]
    ml_width = ml_heads * ml_head_dim
    n_gates = GATES_PER_HEAD * ml_heads
    da_v_dim = da_subln_g.shape[1]
    da_qk_dim = da_lambda.shape[2]
    da_width = w_branch_b.shape[1]
    da_heads = da_width // da_v_dim
    assert ml_width == d and da_width == d and da_v_dim == LANES and 2 * da_qk_dim == LANES
    assert w_in.shape[2] == 11 * d + n_gates and ml_gate_b.shape[1] == GATES_PER_HEAD
    n = batch * seq

    col = {name: i for i, name in enumerate(
        ("a_q", "a_k", "a_o", "a_z", "b_q", "b_k", "b_z", "g_a", "g_b"))}
    bv0 = 7 * d + n_gates

    tiles = _tiles(seq)
    tm_proj, tm_prep, tm_merge = tiles["in_proj"], tiles["prep"], tiles["merge"]
    tq, tk = tiles["attn_q"], tiles["attn_k"]

    x2 = x.reshape(n, d)
    pos2 = positions.reshape(n, 1)
    inv_freq = ROPE_THETA ** (-jnp.arange(0, da_qk_dim, 2, dtype=F32) / da_qk_dim)
    freq_tile = jnp.tile(inv_freq, LANES // inv_freq.shape[0]).reshape(1, LANES)

    for layer in range(depth):
        w = w_in[layer]
        w_qk = w[:, 5 * d + n_gates:bv0].reshape(d, 2 * da_heads, 2, 2, da_qk_dim // 2)
        w_qk = w_qk.transpose(0, 1, 3, 2, 4).reshape(d, 2 * da_width)
        w_main = jnp.concatenate([w[:, :2 * d], w[:, 3 * d:5 * d], w_qk, w[:, bv0 + d:]],
                                 axis=1).astype(BF16)
        wvt = w[:, bv0:bv0 + d].T.astype(BF16)
        wavt = w[:, 2 * d:3 * d].T.astype(BF16)
        w_g = w[:, 5 * d:5 * d + n_gates].reshape(d, GATES_PER_HEAD, ml_heads).transpose(0, 2, 1)
        w_g = w_g.reshape(d, n_gates).astype(BF16)
        wg = jnp.pad(w_g, ((0, 0), (0, LANES - n_gates)))
        bias_hg = ml_gate_b[layer].astype(F32).T
        bias_row = jnp.pad(bias_hg.reshape(1, n_gates), ((0, 0), (0, LANES - n_gates)))
        bias_col = bias_hg.reshape(n_gates, 1)

        p, gates, gates_t, vt, avt = _in_proj(
            x2, norm_g[layer].reshape(1, d), w_main, wg, w_g.T, wvt, wavt, pos2, freq_tile,
            tm=tm_proj, tn=d, tk=tk, rope_cols=(col["b_q"], col["b_k"]),
            q_scale=da_qk_dim ** -0.5 * math.log2(math.e))
        qk = _prep(p, ml_conv_w[layer].astype(F32), seq=seq, ml_width=ml_width, tm=tm_prep,
                   k_scale=ml_head_dim ** -0.5)
        h_fw, h_bw = _mlstm(qk, avt, gates, gates_t, bias_row, bias_col,
                            batch=batch, seq=seq, heads=ml_heads, head_dim=ml_head_dim)
        lambda_init = 0.8 - 0.6 * math.exp(-0.3 * layer)
        y_b = _diff_attn(da_lambda[layer].astype(F32), da_subln_g[layer].reshape(1, LANES), vt, p,
                         batch=batch, seq=seq, heads=da_heads, q_col=col["b_q"] * da_heads,
                         k_col=col["b_k"] * da_heads, z_col=col["b_z"] * da_heads, tq=tq,
                         lambda_init=lambda_init)
        x2 = _merge(h_fw, h_bw, p, y_b, x2, ml_norm_g[layer].reshape(1, ml_width), gate_b[layer],
                    final_g.reshape(1, d), w_branch_a[layer].astype(BF16),
                    w_branch_b[layer].astype(BF16), w_out[layer].astype(BF16), tm=tm_merge,
                    head_dim=ml_head_dim, o_col=col["a_o"], z_col=col["a_z"], ga_col=col["g_a"],
                    gb_col=col["g_b"], final_norm=(layer == depth - 1))
    return x2.reshape(batch, seq, d)
```

```python
import functools
import math

import jax
import jax.numpy as jnp
from jax import lax
from jax.experimental import pallas as pl
from jax.experimental.pallas import tpu as pltpu

F32 = jnp.float32
BF16 = jnp.bfloat16

NORM_EPS = 1e-6
ROPE_THETA = 10000.0
GATES_PER_HEAD = 4
LANES = 128
SUBLANES = 8
ONES_ROWS = 16
MXU_WIDTH = 256
V7X_VMEM_BYTES = 64 * 1024 * 1024
VMEM_LIMIT_BYTES = V7X_VMEM_BYTES - 8 * 1024 * 1024

ML_CHUNK = MXU_WIDTH
CONV_ROWS = 128
CONV_SLAB_LANES = 256
ATTN_UNROLL_PAIRS = 2

_NT = (((1,), (1,)), ((), ()))


def _tiles(seq):
    t = dict(in_proj=min(1024, seq), prep=min(512, seq), attn_q=min(2048, seq),
             attn_k=min(512, seq), merge=min(512, seq))
    assert all(seq % v == 0 for v in t.values()) and seq % ML_CHUNK == 0
    assert (seq // t["attn_k"]) % 2 == 0 and t["in_proj"] % t["attn_k"] == 0
    return t


def _params(n_axes):
    return pltpu.CompilerParams(
        dimension_semantics=("arbitrary",) * n_axes,
        vmem_limit_bytes=VMEM_LIMIT_BYTES)


def _sigmoid(x):
    return 1.0 / (1.0 + jnp.exp2(x * (-math.log2(math.e))))


def _silu(x):
    return x * _sigmoid(x)


def _log_sigmoid(x):
    return jnp.minimum(x, 0.0) - jnp.log1p(jnp.exp(-jnp.abs(x)))


def _rope_tables(pos_ref, freq_ref):
    tm = pos_ref.shape[0]
    d_map = LANES // 2
    n_freq = d_map // 2
    groups = LANES // n_freq
    rows_g = tm // groups
    lane_c = lax.broadcasted_iota(jnp.int32, (rows_g, LANES), 1)
    pos_c = pos_ref[0:rows_g, :]
    for g in range(1, groups):
        pos_c = jnp.where(lane_c >= g * n_freq, pos_ref[g * rows_g:(g + 1) * rows_g, :], pos_c)
    ang_c = pos_c.astype(F32) * freq_ref[...]

    def spread(table_c):
        parts = []
        for g in range(groups):
            base = table_c if g == 0 else pltpu.roll(table_c, LANES - g * n_freq, 1)
            full = base
            for r in range(1, groups):
                full = jnp.where(lane_c >= r * n_freq, pltpu.roll(base, r * n_freq, 1), full)
            parts.append(full)
        return jnp.concatenate(parts, axis=0)

    lane = lax.broadcasted_iota(jnp.int32, (tm, LANES), 1)
    sin = spread(jnp.sin(ang_c))
    return spread(jnp.cos(ang_c)), jnp.where(lane < LANES // 2, -sin, sin)


def _in_proj_body(x_ref, g_ref, w_ref, wg_ref, wgt_ref, wvt_ref, wavt_ref, pos_ref, freq_ref,
                  p_ref, gates_ref, gatest_ref, vt_ref, avt_ref, h_scr, cos_scr, sin_scr,
                  *, rope_cols, q_scale):
    j = pl.program_id(1)

    @pl.when(j == 0)
    def _():
        cos_scr[...], sin_scr[...] = _rope_tables(pos_ref, freq_ref)
        x = x_ref[...]
        ms = jnp.mean(x * x, axis=-1, keepdims=True)
        h = (x * lax.rsqrt(ms + NORM_EPS) * g_ref[...]).astype(BF16)
        h_scr[...] = h
        gates_ref[...] = jnp.dot(h, wg_ref[...], preferred_element_type=F32)
        gatest_ref[...] = lax.dot_general(wgt_ref[...], h, _NT, preferred_element_type=F32)
        avt_ref[...] = lax.dot_general(wavt_ref[...], h, _NT,
                                       preferred_element_type=F32).astype(BF16)
        vt = lax.dot_general(wvt_ref[...], h, _NT, preferred_element_type=F32).astype(BF16)
        tk = vt_ref.shape[2]
        for c in range(vt_ref.shape[0]):
            vt_ref[c] = vt[:, c * tk:(c + 1) * tk]

    is_rope = functools.reduce(jnp.logical_or, [j == c for c in rope_cols])

    @pl.when(jnp.logical_not(is_rope))
    def _():
        p_ref[...] = jnp.dot(h_scr[...], w_ref[...], preferred_element_type=F32).astype(BF16)

    @pl.when(is_rope)
    def _():
        r = jnp.dot(h_scr[...], w_ref[...], preferred_element_type=F32)
        scale = jnp.where(j == rope_cols[0], q_scale, 1.0)
        cos = cos_scr[...] * scale
        sin = sin_scr[...] * scale
        for c0 in range(0, r.shape[1], LANES):
            t = r[:, c0:c0 + LANES]
            p_ref[:, c0:c0 + LANES] = (t * cos + pltpu.roll(t, LANES // 2, 1) * sin).astype(BF16)


def _in_proj(x2, norm_g, w_main, wg, wgt, wvt, wavt, pos2, freq_tile, *, tm, tn, tk, rope_cols,
             q_scale):
    n, d = x2.shape
    width = w_main.shape[1]
    ng = wgt.shape[0]
    dv = wvt.shape[0]
    dav = wavt.shape[0]
    return pl.pallas_call(
        functools.partial(_in_proj_body, rope_cols=rope_cols, q_scale=q_scale),
        grid=(n // tm, width // tn),
        in_specs=[
            pl.BlockSpec((tm, d), lambda i, j: (i, 0)),
            pl.BlockSpec((1, d), lambda i, j: (0, 0)),
            pl.BlockSpec((d, tn), lambda i, j: (0, j)),
            pl.BlockSpec((d, LANES), lambda i, j: (0, 0)),
            pl.BlockSpec((ng, d), lambda i, j: (0, 0)),
            pl.BlockSpec((dv, d), lambda i, j: (0, 0)),
            pl.BlockSpec((dav, d), lambda i, j: (0, 0)),
            pl.BlockSpec((tm, 1), lambda i, j: (i, 0)),
            pl.BlockSpec((1, LANES), lambda i, j: (0, 0)),
        ],
        out_specs=[
            pl.BlockSpec((tm, tn), lambda i, j: (i, j)),
            pl.BlockSpec((tm, LANES), lambda i, j: (i, 0)),
            pl.BlockSpec((ng, tm), lambda i, j: (0, i)),
            pl.BlockSpec((tm // tk, dv, tk), lambda i, j: (i, 0, 0)),
            pl.BlockSpec((dav, tm), lambda i, j: (0, i)),
        ],
        out_shape=[
            jax.ShapeDtypeStruct((n, width), BF16),
            jax.ShapeDtypeStruct((n, LANES), F32),
            jax.ShapeDtypeStruct((ng, n), F32),
            jax.ShapeDtypeStruct((n // tk, dv, tk), BF16),
            jax.ShapeDtypeStruct((dav, n), BF16),
        ],
        scratch_shapes=[pltpu.VMEM((tm, d), BF16), pltpu.VMEM((tm, LANES), F32),
                        pltpu.VMEM((tm, LANES), F32)],
        compiler_params=_params(2),
        name="in_proj",
    )(x2, norm_g, w_main, wg, wgt, wvt, wavt, pos2, freq_tile)


def _conv_shift_matrix(conv_width):
    halo, pad, win = SUBLANES, conv_width // 2, CONV_ROWS + 2 * SUBLANES
    k_pad = -(-conv_width * win // MXU_WIDTH) * MXU_WIDTH
    r = jnp.arange(CONV_ROWS)[:, None]
    c = jnp.arange(k_pad)[None, :]
    hit = functools.reduce(jnp.logical_or,
                           [c == t * win + r + t + halo - pad for t in range(conv_width)])
    return hit.astype(BF16)


def _prep_body(qk_ref, prev_ref, next_ref, cw_ref, shift_ref, qk_out, ext_scr, *, tiles_per_seq,
               conv_width, k_scale):
    i = pl.program_id(0)
    tm = qk_ref.shape[0]
    width = qk_ref.shape[1]
    halo = SUBLANES
    win = CONV_ROWS + 2 * halo
    t_in_seq = i % tiles_per_seq
    has_prev = (t_in_seq > 0).astype(F32)
    has_next = (t_in_seq < tiles_per_seq - 1).astype(F32)
    shift = shift_ref[...]
    slab = ext_scr.shape[1]
    k_fill = jnp.zeros((shift.shape[1] - conv_width * win, slab), BF16)

    for c0 in range(0, width, slab):
        cols = slice(c0, c0 + slab)
        ext_scr[0:halo, :] = prev_ref[:, cols].astype(F32) * has_prev
        ext_scr[halo:halo + tm, :] = qk_ref[:, cols].astype(F32)
        ext_scr[halo + tm:halo + tm + halo, :] = next_ref[:, cols].astype(F32) * has_next
        scale = 1.0 if c0 < width // 2 else k_scale
        taps = [cw_ref[t:t + 1, cols] for t in range(conv_width)]
        for r0 in range(0, tm, CONV_ROWS):
            window = ext_scr[r0:r0 + win, :]
            prods = jnp.concatenate([(window * tap).astype(BF16) for tap in taps] + [k_fill], axis=0)
            acc = jnp.dot(shift, prods, preferred_element_type=F32)
            qk_out[r0:r0 + CONV_ROWS, cols] = (_silu(acc) * scale).astype(BF16)


def _prep(p, conv_w, *, seq, ml_width, tm, k_scale):
    n = p.shape[0]
    n_tiles = n // tm
    halo_blocks = n // SUBLANES
    qk_w = 2 * ml_width
    assert conv_w.shape[0] // 2 <= SUBLANES
    shift = _conv_shift_matrix(conv_w.shape[0])
    body = functools.partial(
        _prep_body, tiles_per_seq=seq // tm, conv_width=conv_w.shape[0], k_scale=k_scale)
    per_tile = tm // SUBLANES
    return pl.pallas_call(
        body,
        grid=(n_tiles,),
        in_specs=[
            pl.BlockSpec((tm, qk_w), lambda i: (i, 0)),
            pl.BlockSpec((SUBLANES, qk_w), lambda i: (jnp.maximum(i * per_tile - 1, 0), 0)),
            pl.BlockSpec((SUBLANES, qk_w),
                         lambda i: (jnp.minimum((i + 1) * per_tile, halo_blocks - 1), 0)),
            pl.BlockSpec((conv_w.shape[0], qk_w), lambda i: (0, 0)),
            pl.BlockSpec(shift.shape, lambda i: (0, 0)),
        ],
        out_specs=pl.BlockSpec((tm, qk_w), lambda i: (i, 0)),
        out_shape=jax.ShapeDtypeStruct((n, qk_w), BF16),
        scratch_shapes=[pltpu.VMEM((tm + 2 * SUBLANES, CONV_SLAB_LANES), F32)],
        compiler_params=_params(1),
        name="prep",
    )(p, p, p, conv_w, shift)


def _split3(f):
    hi = f.astype(BF16)
    r1 = f - hi.astype(F32)
    mid = r1.astype(BF16)
    lo = (r1 - mid.astype(F32)).astype(BF16)
    return hi, mid, lo


def _mlstm_chunk(qc, kc, vt, i_row, b_row, a_col, valid_t, c_ref, m_ref, forward):
    L = qc.shape[0]
    d = vt.shape[0]
    a_row = i_row - b_row
    m_st = m_ref[...]
    log_d = jnp.where(valid_t, a_col + b_row, -jnp.inf)
    m_t = jnp.maximum(b_row + m_st, jnp.max(log_d, axis=0, keepdims=True))
    w_intra = lax.dot_general(kc, qc, _NT, preferred_element_type=F32) * jnp.exp(log_d - m_t)
    s_inter = jnp.exp(b_row + m_st - m_t)
    c_aug = c_ref[...]
    inter = lax.dot_general(c_aug.astype(BF16), qc, _NT, preferred_element_type=F32)
    vt_aug = jnp.concatenate([vt, jnp.ones((ONES_ROWS, L), BF16)], axis=0)
    intra = jnp.dot(vt_aug, w_intra.astype(BF16), preferred_element_type=F32)
    num = intra[:d] + s_inter * inter[:d]
    den = intra[d:d + 1] + s_inter * inter[d:d + 1]
    h_t = num / jnp.maximum(jnp.abs(den), jnp.exp(-m_t))

    b_last = b_row[:, L - 1:L] if forward else b_row[:, 0:1]
    m_new = jnp.maximum(b_last + m_st, jnp.max(b_last + a_row, axis=-1, keepdims=True))
    w_row = jnp.exp(b_last + a_row - m_new)
    decay = jnp.exp(b_last + m_st - m_new)
    w_b = w_row.astype(BF16)
    lhs = jnp.concatenate([vt * w_b, jnp.broadcast_to(w_b, (ONES_ROWS, L))], axis=0)
    c_ref[...] = decay * c_aug + jnp.dot(lhs, kc, preferred_element_type=F32)
    m_ref[...] = m_new
    return h_t.T


def _mlstm_body(qf_ref, kf_ref, vf_ref, gf_ref, gtf_ref, qb_ref, kb_ref, vb_ref, gb_ref, gtb_ref,
                brow_ref, bcol_ref, hf_ref, hb_ref, c_scr, m_scr, *, heads):
    @pl.when(pl.program_id(1) == 0)
    def _():
        for ref in (c_scr, m_scr):
            ref[...] = jnp.zeros(ref.shape, ref.dtype)

    L = qf_ref.shape[0]
    d = qf_ref.shape[1] // heads
    r = lax.broadcasted_iota(jnp.int32, (L, L), 0)
    c = lax.broadcasted_iota(jnp.int32, (L, L), 1)
    lower, upper = c <= r, c >= r
    tri = {True: lower.astype(BF16), False: upper.astype(BF16)}

    for forward, q_ref, k_ref, v_ref, g_ref, gt_ref, out_ref in (
            (True, qf_ref, kf_ref, vf_ref, gf_ref, gtf_ref, hf_ref),
            (False, qb_ref, kb_ref, vb_ref, gb_ref, gtb_ref, hb_ref)):
        gi = 0 if forward else 2
        g = g_ref[...] + brow_ref[...]
        gt = gt_ref[...] + bcol_ref[...]
        cum_col = sum(jnp.dot(tri[forward], part, preferred_element_type=F32)
                      for part in _split3(_log_sigmoid(g)))
        cum_row = sum(jnp.dot(part, tri[not forward], preferred_element_type=F32)
                      for part in _split3(_log_sigmoid(gt)))
        valid_t = upper if forward else lower
        for h in range(heads):
            ii, fi = h * GATES_PER_HEAD + gi, h * GATES_PER_HEAD + gi + 1
            cols = slice(h * d, (h + 1) * d)
            ci = h if forward else heads + h
            out_ref[:, cols] = _mlstm_chunk(
                q_ref[:, cols], k_ref[:, cols], v_ref[cols, :],
                gt[ii:ii + 1, :], cum_row[fi:fi + 1, :], g[:, ii:ii + 1] - cum_col[:, fi:fi + 1],
                valid_t, c_scr.at[ci], m_scr.at[ci], forward)


def _mlstm(qk, avt, gates, gates_t, bias_row, bias_col, *, batch, seq, heads, head_dim):
    n = qk.shape[0]
    L = ML_CHUNK
    nc = seq // L
    width = heads * head_dim
    n_gates = gates_t.shape[0]
    fw = lambda b, c: b * nc + c
    bw = lambda b, c: b * nc + (nc - 1 - c)

    def specs(rowf):
        return [
            pl.BlockSpec((L, width), lambda b, c: (rowf(b, c), 0)),
            pl.BlockSpec((L, width), lambda b, c: (rowf(b, c), 1)),
            pl.BlockSpec((width, L), lambda b, c: (0, rowf(b, c))),
            pl.BlockSpec((L, LANES), lambda b, c: (rowf(b, c), 0)),
            pl.BlockSpec((n_gates, L), lambda b, c: (0, rowf(b, c))),
        ]

    def call_args():
        return [qk, qk, avt, gates, gates_t]

    return pl.pallas_call(
        functools.partial(_mlstm_body, heads=heads),
        grid=(batch, nc),
        in_specs=specs(fw) + specs(bw) + [
            pl.BlockSpec((1, LANES), lambda b, c: (0, 0)),
            pl.BlockSpec((n_gates, 1), lambda b, c: (0, 0)),
        ],
        out_specs=[pl.BlockSpec((L, width), lambda b, c: (fw(b, c), 0)),
                   pl.BlockSpec((L, width), lambda b, c: (bw(b, c), 0))],
        out_shape=[jax.ShapeDtypeStruct((n, width), F32)] * 2,
        scratch_shapes=[
            pltpu.VMEM((2 * heads, head_dim + ONES_ROWS, head_dim), F32),
            pltpu.VMEM((2 * heads, 1, 1), F32),
        ],
        compiler_params=_params(2),
        name="mlstm",
    )(*call_args(), *call_args(), bias_row, bias_col)


def _attn_body(lam_ref, g_ref, q_ref, k_ref, vt_ref, bz_ref, o_ref, m_scr, acc_scr,
               s_scr0, s_scr1, p_scr0, p_scr1, a_scr0, a_scr1, x_scr0, x_scr1, *, lambda_init):
    tq = q_ref.shape[0]
    n_chunks, d_v, tk = vt_ref.shape
    q = q_ref[...]
    lane = lax.broadcasted_iota(jnp.int32, (tq, LANES), 1)
    zero = jnp.zeros_like(q)
    first = (lane % (LANES // 2)) < LANES // 4
    q_maps = (jnp.where(first, q, zero), jnp.where(first, zero, q))
    s_scr, p_scr, a_scr = (s_scr0, s_scr1), (p_scr0, p_scr1), (a_scr0, a_scr1)
    x_scr = (x_scr0, x_scr1)
    ones_rows = jnp.ones((acc_scr.shape[1] - d_v, tk), BF16)

    m_scr[...] = jnp.full(m_scr.shape, -jnp.inf, F32)
    acc_scr[...] = jnp.zeros(acc_scr.shape, F32)

    def scores(j, slot):
        k = k_ref[pl.ds(pl.multiple_of(j * tk, tk), tk), :]
        for mi in range(2):
            s = lax.dot_general(k, q_maps[mi], _NT, preferred_element_type=F32)
            s_scr[slot][mi] = s
            x_scr[slot][mi] = jnp.max(s, axis=0, keepdims=True)

    def softmax(slot):
        for mi in range(2):
            m_prev = m_scr[mi]
            m_cur = jnp.maximum(m_prev, x_scr[slot][mi])
            m_scr[mi] = m_cur
            a_scr[slot][mi] = jnp.exp2(m_prev - m_cur)
            p_scr[slot][mi] = jnp.exp2(s_scr[slot][mi] - m_cur).astype(BF16)

    def values(j, slot):
        vt = jnp.concatenate([vt_ref[j], ones_rows], axis=0)
        for mi in range(2):
            acc_scr[mi] = a_scr[slot][mi] * acc_scr[mi] + jnp.dot(
                vt, p_scr[slot][mi], preferred_element_type=F32)

    def two_steps(t):
        j = 2 * t + 1
        scores(j + 1, 0)
        softmax(1)
        values(j - 1, 0)
        scores(j + 2, 1)
        softmax(0)
        values(j, 1)

    scores(0, 0)
    scores(1, 1)
    softmax(0)
    n_pairs = (n_chunks - 2) // 2
    unroll = min(ATTN_UNROLL_PAIRS, max(n_pairs, 1))
    head = n_pairs % unroll
    for t in range(head):
        two_steps(t)

    def loop_body(i, carry):
        for u in range(unroll):
            two_steps(head + i * unroll + u)
        return carry

    lax.fori_loop(0, n_pairs // unroll, loop_body, 0)
    softmax(1)
    values(n_chunks - 2, 0)
    values(n_chunks - 1, 1)

    lp = lam_ref[...]
    dots = jnp.sum(lp[0:1, :] * lp[1:2, :], axis=-1, keepdims=True), \
        jnp.sum(lp[2:3, :] * lp[3:4, :], axis=-1, keepdims=True)
    lam = jnp.exp(dots[0]) - jnp.exp(dots[1]) + lambda_init
    acc0, acc1 = acc_scr[0], acc_scr[1]
    att = (acc0[:d_v] / acc0[d_v:d_v + 1] - lam * (acc1[:d_v] / acc1[d_v:d_v + 1]))
    ms = jnp.mean(att * att, axis=0, keepdims=True)
    att = (att * lax.rsqrt(ms + NORM_EPS)).T
    att = att * g_ref[...] * (1.0 - lambda_init)
    o_ref[...] = (att * _silu(bz_ref[...].astype(F32))).astype(BF16)


def _diff_attn(lam_p, subln_g, vt, p, *, batch, seq, heads, q_col, k_col, z_col, tq, lambda_init):
    n = p.shape[0]
    nq = seq // tq
    tk = vt.shape[2]
    nk = seq // tk
    body = functools.partial(_attn_body, lambda_init=lambda_init)
    return pl.pallas_call(
        body,
        grid=(batch, heads, nq),
        in_specs=[
            pl.BlockSpec(lam_p.shape, lambda b, h, i: (0, 0)),
            pl.BlockSpec((1, LANES), lambda b, h, i: (0, 0)),
            pl.BlockSpec((tq, LANES), lambda b, h, i: (b * nq + i, q_col + h)),
            pl.BlockSpec((seq, LANES), lambda b, h, i: (b, k_col + h)),
            pl.BlockSpec((nk, LANES, tk), lambda b, h, i: (b, h, 0)),
            pl.BlockSpec((tq, LANES), lambda b, h, i: (b * nq + i, z_col + h)),
        ],
        out_specs=pl.BlockSpec((tq, LANES), lambda b, h, i: (b * nq + i, h)),
        out_shape=jax.ShapeDtypeStruct((n, heads * LANES), BF16),
        scratch_shapes=[
            pltpu.VMEM((2, 1, tq), F32),
            pltpu.VMEM((2, LANES + ONES_ROWS, tq), F32),
            pltpu.VMEM((2, tk, tq), F32), pltpu.VMEM((2, tk, tq), F32),
            pltpu.VMEM((2, tk, tq), BF16), pltpu.VMEM((2, tk, tq), BF16),
            pltpu.VMEM((2, 1, tq), F32), pltpu.VMEM((2, 1, tq), F32),
            pltpu.VMEM((2, 1, tq), F32), pltpu.VMEM((2, 1, tq), F32),
        ],
        compiler_params=_params(3),
        name="diff_attn",
    )(lam_p, subln_g, p, p, vt, p)


def _merge_body(hf_ref, hb_ref, ao_ref, az_ref, yb_ref, ga_ref, gb_ref, x_ref, mlg_ref, gbias_ref,
                fing_ref, wa_ref, wb_ref, wo_ref, o_ref, *, head_dim, final_norm):
    hs = hf_ref[...] + hb_ref[...]
    width = hs.shape[1]
    parts = []
    for c0 in range(0, width, head_dim):
        hh = hs[:, c0:c0 + head_dim]
        ms = jnp.mean(hh * hh, axis=-1, keepdims=True)
        parts.append(hh * lax.rsqrt(ms + NORM_EPS))
    hm = jnp.concatenate(parts, axis=-1) * mlg_ref[...]
    y_a = hm * _sigmoid(ao_ref[...].astype(F32)) * _silu(az_ref[...].astype(F32))
    pa = jnp.dot(y_a.astype(BF16), wa_ref[...], preferred_element_type=F32)
    pb = jnp.dot(yb_ref[...], wb_ref[...], preferred_element_type=F32)
    gbias = gbias_ref[...]
    mix = (_sigmoid(ga_ref[...].astype(F32) + gbias[0:1, :]) * pa
           + _sigmoid(gb_ref[...].astype(F32) + gbias[1:2, :]) * pb)
    xn = x_ref[...] + jnp.dot(mix.astype(BF16), wo_ref[...], preferred_element_type=F32)
    if final_norm:
        ms = jnp.mean(xn * xn, axis=-1, keepdims=True)
        xn = xn * lax.rsqrt(ms + NORM_EPS) * fing_ref[...]
    o_ref[...] = xn


def _merge(h_fw, h_bw, p, y_b, x2, ml_g, gate_bias, final_g, w_a, w_b, w_o, *, tm, head_dim,
           o_col, z_col, ga_col, gb_col, final_norm):
    n, d = x2.shape
    row = lambda c: pl.BlockSpec((tm, d), lambda i: (i, c))
    const = lambda shape: pl.BlockSpec(shape, lambda i: (0, 0))
    body = functools.partial(_merge_body, head_dim=head_dim, final_norm=final_norm)
    return pl.pallas_call(
        body,
        grid=(n // tm,),
        in_specs=[row(0), row(0), row(o_col), row(z_col), row(0), row(ga_col), row(gb_col), row(0),
                  const((1, d)), const((2, d)), const((1, d)),
                  const((d, d)), const((d, d)), const((d, d))],
        out_specs=row(0),
        out_shape=jax.ShapeDtypeStruct((n, d), F32),
        compiler_params=_params(1),
        name="merge",
    )(h_fw, h_bw, p, p, y_b, p, p, x2, ml_g, gate_bias, final_g, w_a, w_b, w_o)


def kernel(x, positions, norm_g, w_in, ml_gate_b, ml_conv_w, ml_norm_g, da_lambda, da_subln_g,
           gate_b, w_branch_a, w_branch_b, w_out, final_g):
    batch, seq, d = x.shape
    depth = w_in.shape[0]
    ml_heads, ml_head_dim = ml_norm_g.shape[1], ml_norm_g.shape[2]
    ml_width = ml_heads * ml_head_dim
    n_gates = GATES_PER_HEAD * ml_heads
    da_v_dim = da_subln_g.shape[1]
    da_qk_dim = da_lambda.shape[2]
    da_width = w_branch_b.shape[1]
    da_heads = da_width // da_v_dim
    assert ml_width == d and da_width == d and da_v_dim == LANES and 2 * da_qk_dim == LANES
    assert w_in.shape[2] == 11 * d + n_gates and ml_gate_b.shape[1] == GATES_PER_HEAD
    n = batch * seq

    col = {name: i for i, name in enumerate(
        ("a_q", "a_k", "a_o", "a_z", "b_q", "b_k", "b_z", "g_a", "g_b"))}
    bv0 = 7 * d + n_gates

    tiles = _tiles(seq)
    tm_proj, tm_prep, tm_merge = tiles["in_proj"], tiles["prep"], tiles["merge"]
    tq, tk = tiles["attn_q"], tiles["attn_k"]

    x2 = x.reshape(n, d)
    pos2 = positions.reshape(n, 1)
    inv_freq = ROPE_THETA ** (-jnp.arange(0, da_qk_dim, 2, dtype=F32) / da_qk_dim)
    freq_tile = jnp.tile(inv_freq, LANES // inv_freq.shape[0]).reshape(1, LANES)

    for layer in range(depth):
        w = w_in[layer]
        w_qk = w[:, 5 * d + n_gates:bv0].reshape(d, 2 * da_heads, 2, 2, da_qk_dim // 2)
        w_qk = w_qk.transpose(0, 1, 3, 2, 4).reshape(d, 2 * da_width)
        w_main = jnp.concatenate([w[:, :2 * d], w[:, 3 * d:5 * d], w_qk, w[:, bv0 + d:]],
                                 axis=1).astype(BF16)
        wvt = w[:, bv0:bv0 + d].T.astype(BF16)
        wavt = w[:, 2 * d:3 * d].T.astype(BF16)
        w_g = w[:, 5 * d:5 * d + n_gates].reshape(d, GATES_PER_HEAD, ml_heads).transpose(0, 2, 1)
        w_g = w_g.reshape(d, n_gates).astype(BF16)
        wg = jnp.pad(w_g, ((0, 0), (0, LANES - n_gates)))
        bias_hg = ml_gate_b[layer].astype(F32).T
        bias_row = jnp.pad(bias_hg.reshape(1, n_gates), ((0, 0), (0, LANES - n_gates)))
        bias_col = bias_hg.reshape(n_gates, 1)

        p, gates, gates_t, vt, avt = _in_proj(
            x2, norm_g[layer].reshape(1, d), w_main, wg, w_g.T, wvt, wavt, pos2, freq_tile,
            tm=tm_proj, tn=d, tk=tk, rope_cols=(col["b_q"], col["b_k"]),
            q_scale=da_qk_dim ** -0.5 * math.log2(math.e))
        qk = _prep(p, ml_conv_w[layer].astype(F32), seq=seq, ml_width=ml_width, tm=tm_prep,
                   k_scale=ml_head_dim ** -0.5)
        h_fw, h_bw = _mlstm(qk, avt, gates, gates_t, bias_row, bias_col,
                            batch=batch, seq=seq, heads=ml_heads, head_dim=ml_head_dim)
        lambda_init = 0.8 - 0.6 * math.exp(-0.3 * layer)
        y_b = _diff_attn(da_lambda[layer].astype(F32), da_subln_g[layer].reshape(1, LANES), vt, p,
                         batch=batch, seq=seq, heads=da_heads, q_col=col["b_q"] * da_heads,
                         k_col=col["b_k"] * da_heads, z_col=col["b_z"] * da_heads, tq=tq,
                         lambda_init=lambda_init)
        x2 = _merge(h_fw, h_bw, p, y_b, x2, ml_norm_g[layer].reshape(1, ml_width), gate_b[layer],
                    final_g.reshape(1, d), w_branch_a[layer].astype(BF16),
                    w_branch_b[layer].astype(BF16), w_out[layer].astype(BF16), tm=tm_merge,
                    head_dim=ml_head_dim, o_col=col["a_o"], z_col=col["a_z"], ga_col=col["g_a"],
                    gb_col=col["g_b"], final_norm=(layer == depth - 1))
    return x2.reshape(batch, seq, d)
```

```python
import functools
import math

import jax
import jax.numpy as jnp
from jax import lax
from jax.experimental import pallas as pl
from jax.experimental.pallas import tpu as pltpu

F32 = jnp.float32
BF16 = jnp.bfloat16

NORM_EPS = 1e-6
ROPE_THETA = 10000.0
GATES_PER_HEAD = 4
LANES = 128
SUBLANES = 8
ONES_ROWS = 16
MXU_WIDTH = 256
V7X_VMEM_BYTES = 64 * 1024 * 1024
VMEM_LIMIT_BYTES = V7X_VMEM_BYTES - 8 * 1024 * 1024

ML_CHUNK = MXU_WIDTH
CONV_ROWS = 128
CONV_SLAB_LANES = 256
ATTN_UNROLL_PAIRS = 2

_NT = (((1,), (1,)), ((), ()))
_TNT = (((0,), (1,)), ((), ()))


def _tiles(seq):
    t = dict(in_proj=min(1024, seq), prep=min(512, seq), attn_q=min(2048, seq),
             attn_k=min(512, seq), merge=min(512, seq))
    assert all(seq % v == 0 for v in t.values()) and seq % ML_CHUNK == 0
    assert (seq // t["attn_k"]) % 2 == 0 and t["in_proj"] % t["attn_k"] == 0
    return t


def _params(n_axes):
    return pltpu.CompilerParams(
        dimension_semantics=("arbitrary",) * n_axes,
        vmem_limit_bytes=VMEM_LIMIT_BYTES)


def _sigmoid(x):
    return 1.0 / (1.0 + jnp.exp2(x * (-math.log2(math.e))))


def _silu(x):
    return x * _sigmoid(x)


def _log_sigmoid(x):
    return jnp.minimum(x, 0.0) - jnp.log1p(jnp.exp(-jnp.abs(x)))


def _rope_tables(pos_ref, freq_ref):
    tm = pos_ref.shape[0]
    d_map = LANES // 2
    n_freq = d_map // 2
    groups = LANES // n_freq
    rows_g = tm // groups
    lane_c = lax.broadcasted_iota(jnp.int32, (rows_g, LANES), 1)
    pos_c = pos_ref[0:rows_g, :]
    for g in range(1, groups):
        pos_c = jnp.where(lane_c >= g * n_freq, pos_ref[g * rows_g:(g + 1) * rows_g, :], pos_c)
    ang_c = pos_c.astype(F32) * freq_ref[...]

    def spread(table_c):
        parts = []
        for g in range(groups):
            base = table_c if g == 0 else pltpu.roll(table_c, LANES - g * n_freq, 1)
            full = base
            for r in range(1, groups):
                full = jnp.where(lane_c >= r * n_freq, pltpu.roll(base, r * n_freq, 1), full)
            parts.append(full)
        return jnp.concatenate(parts, axis=0)

    lane = lax.broadcasted_iota(jnp.int32, (tm, LANES), 1)
    sin = spread(jnp.sin(ang_c))
    return spread(jnp.cos(ang_c)), jnp.where(lane < LANES // 2, -sin, sin)


def _in_proj_body(x_ref, g_ref, w_ref, wg_ref, wgt_ref, wvt_ref, wavt_ref, pos_ref, freq_ref,
                  p_ref, gates_ref, gatest_ref, vt_ref, avt_ref, h_scr, cos_scr, sin_scr,
                  *, rope_cols, q_scale):
    j = pl.program_id(1)

    @pl.when(j == 0)
    def _():
        cos_scr[...], sin_scr[...] = _rope_tables(pos_ref, freq_ref)
        x = x_ref[...]
        ms = jnp.mean(x * x, axis=-1, keepdims=True)
        h = (x * lax.rsqrt(ms + NORM_EPS) * g_ref[...]).astype(BF16)
        h_scr[...] = h
        gates_ref[...] = jnp.dot(h, wg_ref[...], preferred_element_type=F32)
        gatest_ref[...] = lax.dot_general(wgt_ref[...], h, _NT, preferred_element_type=F32)
        avt_ref[...] = lax.dot_general(wavt_ref[...], h, _TNT,
                                       preferred_element_type=F32).astype(BF16)
        vt = lax.dot_general(wvt_ref[...], h, _TNT, preferred_element_type=F32).astype(BF16)
        tk = vt_ref.shape[2]
        for c in range(vt_ref.shape[0]):
            vt_ref[c] = vt[:, c * tk:(c + 1) * tk]

    is_rope = functools.reduce(jnp.logical_or, [j == c for c in rope_cols])

    @pl.when(jnp.logical_not(is_rope))
    def _():
        p_ref[...] = jnp.dot(h_scr[...], w_ref[...], preferred_element_type=F32).astype(BF16)

    @pl.when(is_rope)
    def _():
        r = jnp.dot(h_scr[...], w_ref[...], preferred_element_type=F32)
        scale = jnp.where(j == rope_cols[0], q_scale, 1.0)
        cos = cos_scr[...] * scale
        sin = sin_scr[...] * scale
        for c0 in range(0, r.shape[1], LANES):
            t = r[:, c0:c0 + LANES]
            p_ref[:, c0:c0 + LANES] = (t * cos + pltpu.roll(t, LANES // 2, 1) * sin).astype(BF16)


def _in_proj(x2, norm_g, w_main, wg, wgt, wvt, wavt, pos2, freq_tile, *, tm, tn, tk, rope_cols,
             q_scale):
    n, d = x2.shape
    width = w_main.shape[1]
    ng = wgt.shape[0]
    dv = wvt.shape[1]
    dav = wavt.shape[1]
    return pl.pallas_call(
        functools.partial(_in_proj_body, rope_cols=rope_cols, q_scale=q_scale),
        grid=(n // tm, width // tn),
        in_specs=[
            pl.BlockSpec((tm, d), lambda i, j: (i, 0)),
            pl.BlockSpec((1, d), lambda i, j: (0, 0)),
            pl.BlockSpec((d, tn), lambda i, j: (0, j)),
            pl.BlockSpec((d, LANES), lambda i, j: (0, 0)),
            pl.BlockSpec((ng, d), lambda i, j: (0, 0)),
            pl.BlockSpec((d, dv), lambda i, j: (0, 0)),
            pl.BlockSpec((d, dav), lambda i, j: (0, 0)),
            pl.BlockSpec((tm, 1), lambda i, j: (i, 0)),
            pl.BlockSpec((1, LANES), lambda i, j: (0, 0)),
        ],
        out_specs=[
            pl.BlockSpec((tm, tn), lambda i, j: (i, j)),
            pl.BlockSpec((tm, LANES), lambda i, j: (i, 0)),
            pl.BlockSpec((ng, tm), lambda i, j: (0, i)),
            pl.BlockSpec((tm // tk, dv, tk), lambda i, j: (i, 0, 0)),
            pl.BlockSpec((dav, tm), lambda i, j: (0, i)),
        ],
        out_shape=[
            jax.ShapeDtypeStruct((n, width), BF16),
            jax.ShapeDtypeStruct((n, LANES), F32),
            jax.ShapeDtypeStruct((ng, n), F32),
            jax.ShapeDtypeStruct((n // tk, dv, tk), BF16),
            jax.ShapeDtypeStruct((dav, n), BF16),
        ],
        scratch_shapes=[pltpu.VMEM((tm, d), BF16), pltpu.VMEM((tm, LANES), F32),
                        pltpu.VMEM((tm, LANES), F32)],
        compiler_params=_params(2),
        name="in_proj",
    )(x2, norm_g, w_main, wg, wgt, wvt, wavt, pos2, freq_tile)


def _conv_shift_matrix(conv_width):
    halo, pad, win = SUBLANES, conv_width // 2, CONV_ROWS + 2 * SUBLANES
    k_pad = -(-conv_width * win // MXU_WIDTH) * MXU_WIDTH
    r = jnp.arange(CONV_ROWS)[:, None]
    c = jnp.arange(k_pad)[None, :]
    hit = functools.reduce(jnp.logical_or,
                           [c == t * win + r + t + halo - pad for t in range(conv_width)])
    return hit.astype(BF16)


def _prep_body(qk_ref, prev_ref, next_ref, cw_ref, shift_ref, qk_out, ext_scr, *, tiles_per_seq,
               conv_width, k_scale):
    i = pl.program_id(0)
    tm = qk_ref.shape[0]
    width = qk_ref.shape[1]
    halo = SUBLANES
    win = CONV_ROWS + 2 * halo
    t_in_seq = i % tiles_per_seq
    has_prev = (t_in_seq > 0).astype(F32)
    has_next = (t_in_seq < tiles_per_seq - 1).astype(F32)
    shift = shift_ref[...]
    slab = ext_scr.shape[1]
    k_fill = jnp.zeros((shift.shape[1] - conv_width * win, slab), BF16)

    for c0 in range(0, width, slab):
        cols = slice(c0, c0 + slab)
        ext_scr[0:halo, :] = prev_ref[:, cols].astype(F32) * has_prev
        ext_scr[halo:halo + tm, :] = qk_ref[:, cols].astype(F32)
        ext_scr[halo + tm:halo + tm + halo, :] = next_ref[:, cols].astype(F32) * has_next
        scale = 1.0 if c0 < width // 2 else k_scale
        taps = [cw_ref[t:t + 1, cols] for t in range(conv_width)]
        for r0 in range(0, tm, CONV_ROWS):
            window = ext_scr[r0:r0 + win, :]
            prods = jnp.concatenate([(window * tap).astype(BF16) for tap in taps] + [k_fill], axis=0)
            acc = jnp.dot(shift, prods, preferred_element_type=F32)
            qk_out[r0:r0 + CONV_ROWS, cols] = (_silu(acc) * scale).astype(BF16)


def _prep(p, conv_w, *, seq, ml_width, tm, k_scale):
    n = p.shape[0]
    n_tiles = n // tm
    halo_blocks = n // SUBLANES
    qk_w = 2 * ml_width
    assert conv_w.shape[0] // 2 <= SUBLANES
    shift = _conv_shift_matrix(conv_w.shape[0])
    body = functools.partial(
        _prep_body, tiles_per_seq=seq // tm, conv_width=conv_w.shape[0], k_scale=k_scale)
    per_tile = tm // SUBLANES
    return pl.pallas_call(
        body,
        grid=(n_tiles,),
        in_specs=[
            pl.BlockSpec((tm, qk_w), lambda i: (i, 0)),
            pl.BlockSpec((SUBLANES, qk_w), lambda i: (jnp.maximum(i * per_tile - 1, 0), 0)),
            pl.BlockSpec((SUBLANES, qk_w),
                         lambda i: (jnp.minimum((i + 1) * per_tile, halo_blocks - 1), 0)),
            pl.BlockSpec((conv_w.shape[0], qk_w), lambda i: (0, 0)),
            pl.BlockSpec(shift.shape, lambda i: (0, 0)),
        ],
        out_specs=pl.BlockSpec((tm, qk_w), lambda i: (i, 0)),
        out_shape=jax.ShapeDtypeStruct((n, qk_w), BF16),
        scratch_shapes=[pltpu.VMEM((tm + 2 * SUBLANES, CONV_SLAB_LANES), F32)],
        compiler_params=_params(1),
        name="prep",
    )(p, p, p, conv_w, shift)


def _split3(f):
    hi = f.astype(BF16)
    r1 = f - hi.astype(F32)
    mid = r1.astype(BF16)
    lo = (r1 - mid.astype(F32)).astype(BF16)
    return hi, mid, lo


def _mlstm_chunk(qc, kc, vt, i_row, b_row, a_col, valid_t, c_ref, m_ref, forward):
    L = qc.shape[0]
    d = vt.shape[0]
    a_row = i_row - b_row
    m_st = m_ref[...]
    log_d = jnp.where(valid_t, a_col + b_row, -jnp.inf)
    m_t = jnp.maximum(b_row + m_st, jnp.max(log_d, axis=0, keepdims=True))
    w_intra = lax.dot_general(kc, qc, _NT, preferred_element_type=F32) * jnp.exp(log_d - m_t)
    s_inter = jnp.exp(b_row + m_st - m_t)
    c_aug = c_ref[...]
    inter = lax.dot_general(c_aug.astype(BF16), qc, _NT, preferred_element_type=F32)
    vt_aug = jnp.concatenate([vt, jnp.ones((ONES_ROWS, L), BF16)], axis=0)
    intra = jnp.dot(vt_aug, w_intra.astype(BF16), preferred_element_type=F32)
    num = intra[:d] + s_inter * inter[:d]
    den = intra[d:d + 1] + s_inter * inter[d:d + 1]
    h_t = num / jnp.maximum(jnp.abs(den), jnp.exp(-m_t))

    b_last = b_row[:, L - 1:L] if forward else b_row[:, 0:1]
    m_new = jnp.maximum(b_last + m_st, jnp.max(b_last + a_row, axis=-1, keepdims=True))
    w_row = jnp.exp(b_last + a_row - m_new)
    decay = jnp.exp(b_last + m_st - m_new)
    w_b = w_row.astype(BF16)
    lhs = jnp.concatenate([vt * w_b, jnp.broadcast_to(w_b, (ONES_ROWS, L))], axis=0)
    c_ref[...] = decay * c_aug + jnp.dot(lhs, kc, preferred_element_type=F32)
    m_ref[...] = m_new
    return h_t.T


def _mlstm_body(qf_ref, kf_ref, vf_ref, gf_ref, gtf_ref, qb_ref, kb_ref, vb_ref, gb_ref, gtb_ref,
                brow_ref, bcol_ref, hf_ref, hb_ref, c_scr, m_scr, *, heads):
    @pl.when(pl.program_id(1) == 0)
    def _():
        for ref in (c_scr, m_scr):
            ref[...] = jnp.zeros(ref.shape, ref.dtype)

    L = qf_ref.shape[0]
    d = qf_ref.shape[1] // heads
    r = lax.broadcasted_iota(jnp.int32, (L, L), 0)
    c = lax.broadcasted_iota(jnp.int32, (L, L), 1)
    lower, upper = c <= r, c >= r
    tri = {True: lower.astype(BF16), False: upper.astype(BF16)}

    for forward, q_ref, k_ref, v_ref, g_ref, gt_ref, out_ref in (
            (True, qf_ref, kf_ref, vf_ref, gf_ref, gtf_ref, hf_ref),
            (False, qb_ref, kb_ref, vb_ref, gb_ref, gtb_ref, hb_ref)):
        gi = 0 if forward else 2
        g = g_ref[...] + brow_ref[...]
        gt = gt_ref[...] + bcol_ref[...]
        cum_col = sum(jnp.dot(tri[forward], part, preferred_element_type=F32)
                      for part in _split3(_log_sigmoid(g)))
        cum_row = sum(jnp.dot(part, tri[not forward], preferred_element_type=F32)
                      for part in _split3(_log_sigmoid(gt)))
        valid_t = upper if forward else lower
        for h in range(heads):
            ii, fi = h * GATES_PER_HEAD + gi, h * GATES_PER_HEAD + gi + 1
            cols = slice(h * d, (h + 1) * d)
            ci = h if forward else heads + h
            out_ref[:, cols] = _mlstm_chunk(
                q_ref[:, cols], k_ref[:, cols], v_ref[cols, :],
                gt[ii:ii + 1, :], cum_row[fi:fi + 1, :], g[:, ii:ii + 1] - cum_col[:, fi:fi + 1],
                valid_t, c_scr.at[ci], m_scr.at[ci], forward)


def _mlstm(qk, avt, gates, gates_t, bias_row, bias_col, *, batch, seq, heads, head_dim):
    n = qk.shape[0]
    L = ML_CHUNK
    nc = seq // L
    width = heads * head_dim
    n_gates = gates_t.shape[0]
    fw = lambda b, c: b * nc + c
    bw = lambda b, c: b * nc + (nc - 1 - c)

    def specs(rowf):
        return [
            pl.BlockSpec((L, width), lambda b, c: (rowf(b, c), 0)),
            pl.BlockSpec((L, width), lambda b, c: (rowf(b, c), 1)),
            pl.BlockSpec((width, L), lambda b, c: (0, rowf(b, c))),
            pl.BlockSpec((L, LANES), lambda b, c: (rowf(b, c), 0)),
            pl.BlockSpec((n_gates, L), lambda b, c: (0, rowf(b, c))),
        ]

    def call_args():
        return [qk, qk, avt, gates, gates_t]

    return pl.pallas_call(
        functools.partial(_mlstm_body, heads=heads),
        grid=(batch, nc),
        in_specs=specs(fw) + specs(bw) + [
            pl.BlockSpec((1, LANES), lambda b, c: (0, 0)),
            pl.BlockSpec((n_gates, 1), lambda b, c: (0, 0)),
        ],
        out_specs=[pl.BlockSpec((L, width), lambda b, c: (fw(b, c), 0)),
                   pl.BlockSpec((L, width), lambda b, c: (bw(b, c), 0))],
        out_shape=[jax.ShapeDtypeStruct((n, width), F32)] * 2,
        scratch_shapes=[
            pltpu.VMEM((2 * heads, head_dim + ONES_ROWS, head_dim), F32),
            pltpu.VMEM((2 * heads, 1, 1), F32),
        ],
        compiler_params=_params(2),
        name="mlstm",
    )(*call_args(), *call_args(), bias_row, bias_col)


def _attn_body(lam_ref, g_ref, q_ref, k_ref, vt_ref, bz_ref, o_ref, m_scr, acc_scr,
               s_scr0, s_scr1, p_scr0, p_scr1, a_scr0, a_scr1, x_scr0, x_scr1, *, lambda_init):
    tq = q_ref.shape[0]
    n_chunks, d_v, tk = vt_ref.shape
    q = q_ref[...]
    lane = lax.broadcasted_iota(jnp.int32, (tq, LANES), 1)
    zero = jnp.zeros_like(q)
    first = (lane % (LANES // 2)) < LANES // 4
    q_maps = (jnp.where(first, q, zero), jnp.where(first, zero, q))
    s_scr, p_scr, a_scr = (s_scr0, s_scr1), (p_scr0, p_scr1), (a_scr0, a_scr1)
    x_scr = (x_scr0, x_scr1)
    ones_rows = jnp.ones((acc_scr.shape[1] - d_v, tk), BF16)

    m_scr[...] = jnp.full(m_scr.shape, -jnp.inf, F32)
    acc_scr[...] = jnp.zeros(acc_scr.shape, F32)

    def scores(j, slot):
        k = k_ref[pl.ds(pl.multiple_of(j * tk, tk), tk), :]
        for mi in range(2):
            s = lax.dot_general(k, q_maps[mi], _NT, preferred_element_type=F32)
            s_scr[slot][mi] = s
            x_scr[slot][mi] = jnp.max(s, axis=0, keepdims=True)

    def softmax(slot):
        for mi in range(2):
            m_prev = m_scr[mi]
            m_cur = jnp.maximum(m_prev, x_scr[slot][mi])
            m_scr[mi] = m_cur
            a_scr[slot][mi] = jnp.exp2(m_prev - m_cur)
            p_scr[slot][mi] = jnp.exp2(s_scr[slot][mi] - m_cur).astype(BF16)

    def values(j, slot):
        vt = jnp.concatenate([vt_ref[j], ones_rows], axis=0)
        for mi in range(2):
            acc_scr[mi] = a_scr[slot][mi] * acc_scr[mi] + jnp.dot(
                vt, p_scr[slot][mi], preferred_element_type=F32)

    def two_steps(t):
        j = 2 * t + 1
        scores(j + 1, 0)
        softmax(1)
        values(j - 1, 0)
        scores(j + 2, 1)
        softmax(0)
        values(j, 1)

    scores(0, 0)
    scores(1, 1)
    softmax(0)
    n_pairs = (n_chunks - 2) // 2
    unroll = min(ATTN_UNROLL_PAIRS, max(n_pairs, 1))
    head = n_pairs % unroll
    for t in range(head):
        two_steps(t)

    def loop_body(i, carry):
        for u in range(unroll):
            two_steps(head + i * unroll + u)
        return carry

    lax.fori_loop(0, n_pairs // unroll, loop_body, 0)
    softmax(1)
    values(n_chunks - 2, 0)
    values(n_chunks - 1, 1)

    lp = lam_ref[...]
    dots = jnp.sum(lp[0:1, :] * lp[1:2, :], axis=-1, keepdims=True), \
        jnp.sum(lp[2:3, :] * lp[3:4, :], axis=-1, keepdims=True)
    lam = jnp.exp(dots[0]) - jnp.exp(dots[1]) + lambda_init
    acc0, acc1 = acc_scr[0], acc_scr[1]
    att = (acc0[:d_v] / acc0[d_v:d_v + 1] - lam * (acc1[:d_v] / acc1[d_v:d_v + 1]))
    ms = jnp.mean(att * att, axis=0, keepdims=True)
    att = (att * lax.rsqrt(ms + NORM_EPS)).T
    att = att * g_ref[...] * (1.0 - lambda_init)
    o_ref[...] = (att * _silu(bz_ref[...].astype(F32))).astype(BF16)


def _diff_attn(lam_p, subln_g, vt, p, *, batch, seq, heads, q_col, k_col, z_col, tq, lambda_init):
    n = p.shape[0]
    nq = seq // tq
    tk = vt.shape[2]
    nk = seq // tk
    body = functools.partial(_attn_body, lambda_init=lambda_init)
    return pl.pallas_call(
        body,
        grid=(batch, heads, nq),
        in_specs=[
            pl.BlockSpec(lam_p.shape, lambda b, h, i: (0, 0)),
            pl.BlockSpec((1, LANES), lambda b, h, i: (0, 0)),
            pl.BlockSpec((tq, LANES), lambda b, h, i: (b * nq + i, q_col + h)),
            pl.BlockSpec((seq, LANES), lambda b, h, i: (b, k_col + h)),
            pl.BlockSpec((nk, LANES, tk), lambda b, h, i: (b, h, 0)),
            pl.BlockSpec((tq, LANES), lambda b, h, i: (b * nq + i, z_col + h)),
        ],
        out_specs=pl.BlockSpec((tq, LANES), lambda b, h, i: (b * nq + i, h)),
        out_shape=jax.ShapeDtypeStruct((n, heads * LANES), BF16),
        scratch_shapes=[
            pltpu.VMEM((2, 1, tq), F32),
            pltpu.VMEM((2, LANES + ONES_ROWS, tq), F32),
            pltpu.VMEM((2, tk, tq), F32), pltpu.VMEM((2, tk, tq), F32),
            pltpu.VMEM((2, tk, tq), BF16), pltpu.VMEM((2, tk, tq), BF16),
            pltpu.VMEM((2, 1, tq), F32), pltpu.VMEM((2, 1, tq), F32),
            pltpu.VMEM((2, 1, tq), F32), pltpu.VMEM((2, 1, tq), F32),
        ],
        compiler_params=_params(3),
        name="diff_attn",
    )(lam_p, subln_g, p, p, vt, p)


def _merge_body(hf_ref, hb_ref, ao_ref, az_ref, yb_ref, ga_ref, gb_ref, x_ref, mlg_ref, gbias_ref,
                fing_ref, wa_ref, wb_ref, wo_ref, o_ref, *, head_dim, final_norm):
    hs = hf_ref[...] + hb_ref[...]
    width = hs.shape[1]
    parts = []
    for c0 in range(0, width, head_dim):
        hh = hs[:, c0:c0 + head_dim]
        ms = jnp.mean(hh * hh, axis=-1, keepdims=True)
        parts.append(hh * lax.rsqrt(ms + NORM_EPS))
    hm = jnp.concatenate(parts, axis=-1) * mlg_ref[...]
    y_a = hm * _sigmoid(ao_ref[...].astype(F32)) * _silu(az_ref[...].astype(F32))
    pa = jnp.dot(y_a.astype(BF16), wa_ref[...], preferred_element_type=F32)
    pb = jnp.dot(yb_ref[...], wb_ref[...], preferred_element_type=F32)
    gbias = gbias_ref[...]
    mix = (_sigmoid(ga_ref[...].astype(F32) + gbias[0:1, :]) * pa
           + _sigmoid(gb_ref[...].astype(F32) + gbias[1:2, :]) * pb)
    xn = x_ref[...] + jnp.dot(mix.astype(BF16), wo_ref[...], preferred_element_type=F32)
    if final_norm:
        ms = jnp.mean(xn * xn, axis=-1, keepdims=True)
        xn = xn * lax.rsqrt(ms + NORM_EPS) * fing_ref[...]
    o_ref[...] = xn


def _merge(h_fw, h_bw, p, y_b, x2, ml_g, gate_bias, final_g, w_a, w_b, w_o, *, tm, head_dim,
           o_col, z_col, ga_col, gb_col, final_norm):
    n, d = x2.shape
    row = lambda c: pl.BlockSpec((tm, d), lambda i: (i, c))
    const = lambda shape: pl.BlockSpec(shape, lambda i: (0, 0))
    body = functools.partial(_merge_body, head_dim=head_dim, final_norm=final_norm)
    return pl.pallas_call(
        body,
        grid=(n // tm,),
        in_specs=[row(0), row(0), row(o_col), row(z_col), row(0), row(ga_col), row(gb_col), row(0),
                  const((1, d)), const((2, d)), const((1, d)),
                  const((d, d)), const((d, d)), const((d, d))],
        out_specs=row(0),
        out_shape=jax.ShapeDtypeStruct((n, d), F32),
        compiler_params=_params(1),
        name="merge",
    )(h_fw, h_bw, p, p, y_b, p, p, x2, ml_g, gate_bias, final_g, w_a, w_b, w_o)


def kernel(x, positions, norm_g, w_in, ml_gate_b, ml_conv_w, ml_norm_g, da_lambda, da_subln_g,
           gate_b, w_branch_a, w_branch_b, w_out, final_g):
    batch, seq, d = x.shape
    depth = w_in.shape[0]
    ml_heads, ml_head_dim = ml_norm_g.shape[1], ml_norm_g.shape[2]
    ml_width = ml_heads * ml_head_dim
    n_gates = GATES_PER_HEAD * ml_heads
    da_v_dim = da_subln_g.shape[1]
    da_qk_dim = da_lambda.shape[2]
    da_width = w_branch_b.shape[1]
    da_heads = da_width // da_v_dim
    assert ml_width == d and da_width == d and da_v_dim == LANES and 2 * da_qk_dim == LANES
    assert w_in.shape[2] == 11 * d + n_gates and ml_gate_b.shape[1] == GATES_PER_HEAD
    n = batch * seq

    col = {name: i for i, name in enumerate(
        ("a_q", "a_k", "a_o", "a_z", "b_q", "b_k", "b_z", "g_a", "g_b"))}
    bv0 = 7 * d + n_gates

    tiles = _tiles(seq)
    tm_proj, tm_prep, tm_merge = tiles["in_proj"], tiles["prep"], tiles["merge"]
    tq, tk = tiles["attn_q"], tiles["attn_k"]

    x2 = x.reshape(n, d)
    pos2 = positions.reshape(n, 1)
    inv_freq = ROPE_THETA ** (-jnp.arange(0, da_qk_dim, 2, dtype=F32) / da_qk_dim)
    freq_tile = jnp.tile(inv_freq, LANES // inv_freq.shape[0]).reshape(1, LANES)

    for layer in range(depth):
        w = w_in[layer]
        w_qk = w[:, 5 * d + n_gates:bv0].reshape(d, 2 * da_heads, 2, 2, da_qk_dim // 2)
        w_qk = w_qk.transpose(0, 1, 3, 2, 4).reshape(d, 2 * da_width)
        w_main = jnp.concatenate([w[:, :2 * d], w[:, 3 * d:5 * d], w_qk, w[:, bv0 + d:]],
                                 axis=1).astype(BF16)
        wvt = w[:, bv0:bv0 + d].astype(BF16)
        wavt = w[:, 2 * d:3 * d].astype(BF16)
        w_g = w[:, 5 * d:5 * d + n_gates].reshape(d, GATES_PER_HEAD, ml_heads).transpose(0, 2, 1)
        w_g = w_g.reshape(d, n_gates).astype(BF16)
        wg = jnp.pad(w_g, ((0, 0), (0, LANES - n_gates)))
        bias_hg = ml_gate_b[layer].astype(F32).T
        bias_row = jnp.pad(bias_hg.reshape(1, n_gates), ((0, 0), (0, LANES - n_gates)))
        bias_col = bias_hg.reshape(n_gates, 1)

        p, gates, gates_t, vt, avt = _in_proj(
            x2, norm_g[layer].reshape(1, d), w_main, wg, w_g.T, wvt, wavt, pos2, freq_tile,
            tm=tm_proj, tn=d, tk=tk, rope_cols=(col["b_q"], col["b_k"]),
            q_scale=da_qk_dim ** -0.5 * math.log2(math.e))
        qk = _prep(p, ml_conv_w[layer].astype(F32), seq=seq, ml_width=ml_width, tm=tm_prep,
                   k_scale=ml_head_dim ** -0.5)
        h_fw, h_bw = _mlstm(qk, avt, gates, gates_t, bias_row, bias_col,
                            batch=batch, seq=seq, heads=ml_heads, head_dim=ml_head_dim)
        lambda_init = 0.8 - 0.6 * math.exp(-0.3 * layer)
        y_b = _diff_attn(da_lambda[layer].astype(F32), da_subln_g[layer].reshape(1, LANES), vt, p,
                         batch=batch, seq=seq, heads=da_heads, q_col=col["b_q"] * da_heads,
                         k_col=col["b_k"] * da_heads, z_col=col["b_z"] * da_heads, tq=tq,
                         lambda_init=lambda_init)
        x2 = _merge(h_fw, h_bw, p, y_b, x2, ml_norm_g[layer].reshape(1, ml_width), gate_b[layer],
                    final_g.reshape(1, d), w_branch_a[layer].astype(BF16),
                    w_branch_b[layer].astype(BF16), w_out[layer].astype(BF16), tm=tm_merge,
                    head_dim=ml_head_dim, o_col=col["a_o"], z_col=col["a_z"], ga_col=col["g_a"],
                    gb_col=col["g_b"], final_norm=(layer == depth - 1))
    return x2.reshape(batch, seq, d)
```

```python
import functools
import math

import jax
import jax.numpy as jnp
from jax import lax
from jax.experimental import pallas as pl
from jax.experimental.pallas import tpu as pltpu

F32 = jnp.float32
BF16 = jnp.bfloat16

NORM_EPS = 1e-6
ROPE_THETA = 10000.0
GATES_PER_HEAD = 4
LANES = 128
SUBLANES = 8
ONES_ROWS = 16
MXU_WIDTH = 256
V7X_VMEM_BYTES = 64 * 1024 * 1024
VMEM_LIMIT_BYTES = V7X_VMEM_BYTES - 8 * 1024 * 1024

ML_CHUNK = MXU_WIDTH
CONV_ROWS = 128
CONV_SLAB_LANES = 256
ATTN_UNROLL_PAIRS = 2

_NT = (((1,), (1,)), ((), ()))
_TNT = (((0,), (1,)), ((), ()))


def _tiles(seq):
    t = dict(in_proj=min(1024, seq), prep=min(512, seq), attn_q=min(2048, seq),
             attn_k=min(512, seq), merge=min(512, seq))
    assert all(seq % v == 0 for v in t.values()) and seq % ML_CHUNK == 0
    assert (seq // t["attn_k"]) % 2 == 0 and t["in_proj"] % t["attn_k"] == 0
    return t


def _params(n_axes):
    return pltpu.CompilerParams(
        dimension_semantics=("arbitrary",) * n_axes,
        vmem_limit_bytes=VMEM_LIMIT_BYTES)


def _sigmoid(x):
    return 1.0 / (1.0 + jnp.exp2(x * (-math.log2(math.e))))


def _silu(x):
    return x * _sigmoid(x)


def _log_sigmoid(x):
    return jnp.minimum(x, 0.0) - jnp.log1p(jnp.exp(-jnp.abs(x)))


def _rope_tables(pos_ref, freq_ref):
    tm = pos_ref.shape[0]
    d_map = LANES // 2
    n_freq = d_map // 2
    groups = LANES // n_freq
    rows_g = tm // groups
    lane_c = lax.broadcasted_iota(jnp.int32, (rows_g, LANES), 1)
    pos_c = pos_ref[0:rows_g, :]
    for g in range(1, groups):
        pos_c = jnp.where(lane_c >= g * n_freq, pos_ref[g * rows_g:(g + 1) * rows_g, :], pos_c)
    ang_c = pos_c.astype(F32) * freq_ref[...]

    def spread(table_c):
        parts = []
        for g in range(groups):
            base = table_c if g == 0 else pltpu.roll(table_c, LANES - g * n_freq, 1)
            full = base
            for r in range(1, groups):
                full = jnp.where(lane_c >= r * n_freq, pltpu.roll(base, r * n_freq, 1), full)
            parts.append(full)
        return jnp.concatenate(parts, axis=0)

    lane = lax.broadcasted_iota(jnp.int32, (tm, LANES), 1)
    sin = spread(jnp.sin(ang_c))
    return spread(jnp.cos(ang_c)), jnp.where(lane < LANES // 2, -sin, sin)


def _in_proj_body(x_ref, g_ref, w_ref, wg_ref, wgt_ref, wvt_ref, wavt_ref, pos_ref, freq_ref,
                  p_ref, gates_ref, gatest_ref, vt_ref, avt_ref, h_scr, cos_scr, sin_scr,
                  *, rope_cols, q_scale):
    j = pl.program_id(1)

    @pl.when(j == 0)
    def _():
        cos_scr[...], sin_scr[...] = _rope_tables(pos_ref, freq_ref)
        x = x_ref[...]
        ms = jnp.mean(x * x, axis=-1, keepdims=True)
        h = (x * lax.rsqrt(ms + NORM_EPS) * g_ref[...]).astype(BF16)
        h_scr[...] = h
        gates_ref[...] = jnp.dot(h, wg_ref[...], preferred_element_type=F32)
        gatest_ref[...] = lax.dot_general(wgt_ref[...], h, _NT, preferred_element_type=F32)
        avt_ref[...] = lax.dot_general(wavt_ref[...], h, _TNT,
                                       preferred_element_type=F32).astype(BF16)
        vt = lax.dot_general(wvt_ref[...], h, _TNT, preferred_element_type=F32).astype(BF16)
        tk = vt_ref.shape[2]
        for c in range(vt_ref.shape[0]):
            vt_ref[c] = vt[:, c * tk:(c + 1) * tk]

    is_rope = functools.reduce(jnp.logical_or, [j == c for c in rope_cols])

    @pl.when(jnp.logical_not(is_rope))
    def _():
        p_ref[...] = jnp.dot(h_scr[...], w_ref[...], preferred_element_type=F32).astype(BF16)

    @pl.when(is_rope)
    def _():
        r = jnp.dot(h_scr[...], w_ref[...], preferred_element_type=F32)
        scale = jnp.where(j == rope_cols[0], q_scale, 1.0)
        cos = cos_scr[...] * scale
        sin = sin_scr[...] * scale
        for c0 in range(0, r.shape[1], LANES):
            t = r[:, c0:c0 + LANES]
            p_ref[:, c0:c0 + LANES] = (t * cos + pltpu.roll(t, LANES // 2, 1) * sin).astype(BF16)


def _in_proj(x2, norm_g, w_main, wg, wgt, wvt, wavt, pos2, freq_tile, *, tm, tn, tk, rope_cols,
             q_scale):
    n, d = x2.shape
    width = w_main.shape[1]
    ng = wgt.shape[0]
    dv = wvt.shape[1]
    dav = wavt.shape[1]
    return pl.pallas_call(
        functools.partial(_in_proj_body, rope_cols=rope_cols, q_scale=q_scale),
        grid=(n // tm, width // tn),
        in_specs=[
            pl.BlockSpec((tm, d), lambda i, j: (i, 0)),
            pl.BlockSpec((1, d), lambda i, j: (0, 0)),
            pl.BlockSpec((d, tn), lambda i, j: (0, j)),
            pl.BlockSpec((d, LANES), lambda i, j: (0, 0)),
            pl.BlockSpec((ng, d), lambda i, j: (0, 0)),
            pl.BlockSpec((d, dv), lambda i, j: (0, 0)),
            pl.BlockSpec((d, dav), lambda i, j: (0, 0)),
            pl.BlockSpec((tm, 1), lambda i, j: (i, 0)),
            pl.BlockSpec((1, LANES), lambda i, j: (0, 0)),
        ],
        out_specs=[
            pl.BlockSpec((tm, tn), lambda i, j: (i, j)),
            pl.BlockSpec((tm, LANES), lambda i, j: (i, 0)),
            pl.BlockSpec((ng, tm), lambda i, j: (0, i)),
            pl.BlockSpec((tm // tk, dv, tk), lambda i, j: (i, 0, 0)),
            pl.BlockSpec((dav, tm), lambda i, j: (0, i)),
        ],
        out_shape=[
            jax.ShapeDtypeStruct((n, width), BF16),
            jax.ShapeDtypeStruct((n, LANES), F32),
            jax.ShapeDtypeStruct((ng, n), F32),
            jax.ShapeDtypeStruct((n // tk, dv, tk), BF16),
            jax.ShapeDtypeStruct((dav, n), BF16),
        ],
        scratch_shapes=[pltpu.VMEM((tm, d), BF16), pltpu.VMEM((tm, LANES), F32),
                        pltpu.VMEM((tm, LANES), F32)],
        compiler_params=_params(2),
        name="in_proj",
    )(x2, norm_g, w_main, wg, wgt, wvt, wavt, pos2, freq_tile)


def _conv_shift_matrix(conv_width):
    halo, pad, win = SUBLANES, conv_width // 2, CONV_ROWS + 2 * SUBLANES
    k_pad = -(-conv_width * win // MXU_WIDTH) * MXU_WIDTH
    r = jnp.arange(CONV_ROWS)[:, None]
    c = jnp.arange(k_pad)[None, :]
    hit = functools.reduce(jnp.logical_or,
                           [c == t * win + r + t + halo - pad for t in range(conv_width)])
    return hit.astype(BF16)


def _prep_body(qk_ref, prev_ref, next_ref, cw_ref, shift_ref, qk_out, ext_scr, *, tiles_per_seq,
               conv_width, k_scale):
    i = pl.program_id(0)
    tm = qk_ref.shape[0]
    width = qk_ref.shape[1]
    halo = SUBLANES
    win = CONV_ROWS + 2 * halo
    t_in_seq = i % tiles_per_seq
    has_prev = (t_in_seq > 0).astype(F32)
    has_next = (t_in_seq < tiles_per_seq - 1).astype(F32)
    shift = shift_ref[...]
    slab = ext_scr.shape[1]
    k_fill = jnp.zeros((shift.shape[1] - conv_width * win, slab), BF16)

    for c0 in range(0, width, slab):
        cols = slice(c0, c0 + slab)
        ext_scr[0:halo, :] = prev_ref[:, cols].astype(F32) * has_prev
        ext_scr[halo:halo + tm, :] = qk_ref[:, cols].astype(F32)
        ext_scr[halo + tm:halo + tm + halo, :] = next_ref[:, cols].astype(F32) * has_next
        scale = 1.0 if c0 < width // 2 else k_scale
        taps = [cw_ref[t:t + 1, cols] for t in range(conv_width)]
        for r0 in range(0, tm, CONV_ROWS):
            window = ext_scr[r0:r0 + win, :]
            prods = jnp.concatenate([(window * tap).astype(BF16) for tap in taps] + [k_fill], axis=0)
            acc = jnp.dot(shift, prods, preferred_element_type=F32)
            qk_out[r0:r0 + CONV_ROWS, cols] = (_silu(acc) * scale).astype(BF16)


def _prep(p, conv_w, *, seq, ml_width, tm, k_scale):
    n = p.shape[0]
    n_tiles = n // tm
    halo_blocks = n // SUBLANES
    qk_w = 2 * ml_width
    assert conv_w.shape[0] // 2 <= SUBLANES
    shift = _conv_shift_matrix(conv_w.shape[0])
    body = functools.partial(
        _prep_body, tiles_per_seq=seq // tm, conv_width=conv_w.shape[0], k_scale=k_scale)
    per_tile = tm // SUBLANES
    return pl.pallas_call(
        body,
        grid=(n_tiles,),
        in_specs=[
            pl.BlockSpec((tm, qk_w), lambda i: (i, 0)),
            pl.BlockSpec((SUBLANES, qk_w), lambda i: (jnp.maximum(i * per_tile - 1, 0), 0)),
            pl.BlockSpec((SUBLANES, qk_w),
                         lambda i: (jnp.minimum((i + 1) * per_tile, halo_blocks - 1), 0)),
            pl.BlockSpec((conv_w.shape[0], qk_w), lambda i: (0, 0)),
            pl.BlockSpec(shift.shape, lambda i: (0, 0)),
        ],
        out_specs=pl.BlockSpec((tm, qk_w), lambda i: (i, 0)),
        out_shape=jax.ShapeDtypeStruct((n, qk_w), BF16),
        scratch_shapes=[pltpu.VMEM((tm + 2 * SUBLANES, CONV_SLAB_LANES), F32)],
        compiler_params=_params(1),
        name="prep",
    )(p, p, p, conv_w, shift)


def _split3(f):
    hi = f.astype(BF16)
    r1 = f - hi.astype(F32)
    mid = r1.astype(BF16)
    lo = (r1 - mid.astype(F32)).astype(BF16)
    return hi, mid, lo


def _mlstm_local(qc, kc, vt, i_row, b_row, a_col, valid_t, forward, in_ref, u_ref, r_ref, ml_ref,
                 bl_ref):
    L = qc.shape[0]
    a_row = i_row - b_row
    log_d = jnp.where(valid_t, a_col + b_row, -jnp.inf)
    m_loc = jnp.max(log_d, axis=0, keepdims=True)
    w_intra = lax.dot_general(kc, qc, _NT, preferred_element_type=F32) * jnp.exp(log_d - m_loc)
    vt_aug = jnp.concatenate([vt, jnp.ones((ONES_ROWS, L), BF16)], axis=0)
    in_ref[...] = jnp.dot(vt_aug, w_intra.astype(BF16), preferred_element_type=F32)
    r_ref[0:1, :] = m_loc
    r_ref[1:2, :] = b_row

    b_last = b_row[:, L - 1:L] if forward else b_row[:, 0:1]
    m_w = jnp.max(b_last + a_row, axis=-1, keepdims=True)
    w_b = jnp.exp(b_last + a_row - m_w).astype(BF16)
    lhs = jnp.concatenate([vt * w_b, jnp.broadcast_to(w_b, (ONES_ROWS, L))], axis=0)
    u_ref[...] = jnp.dot(lhs, kc, preferred_element_type=F32)
    ml_ref[...] = m_w
    bl_ref[...] = b_last


def _mlstm_state(qc, in_ref, u_ref, r_ref, ml_ref, bl_ref, c_ref, m_ref):
    d = c_ref.shape[1]
    m_st = m_ref[...]
    m_loc, b_row = r_ref[0:1, :], r_ref[1:2, :]
    m_t = jnp.maximum(b_row + m_st, m_loc)
    c_aug = c_ref[...]
    inter = lax.dot_general(c_aug.astype(BF16), qc, _NT, preferred_element_type=F32)
    s_inter = jnp.exp(b_row + m_st - m_t)
    s_intra = jnp.exp(m_loc - m_t)
    intra = in_ref[...]
    num = s_intra * intra[:d] + s_inter * inter[:d]
    den = s_intra * intra[d:d + 1] + s_inter * inter[d:d + 1]
    h_t = num / jnp.maximum(jnp.abs(den), jnp.exp(-m_t))

    b_last, m_w = bl_ref[...], ml_ref[...]
    m_new = jnp.maximum(b_last + m_st, m_w)
    c_ref[...] = jnp.exp(b_last + m_st - m_new) * c_aug + jnp.exp(m_w - m_new) * u_ref[...]
    m_ref[...] = m_new
    return h_t.T


def _mlstm_body(qf_ref, kf_ref, vf_ref, gf_ref, gtf_ref, qpf_ref, qb_ref, kb_ref, vb_ref, gb_ref,
                gtb_ref, qpb_ref, brow_ref, bcol_ref, hf_ref, hb_ref, c_scr, m_scr, in_scr, u_scr,
                r_scr, ml_scr, bl_scr, *, heads):
    @pl.when(pl.program_id(1) == 0)
    def _():
        for ref in (c_scr, m_scr, in_scr, u_scr, r_scr, ml_scr, bl_scr):
            ref[...] = jnp.zeros(ref.shape, ref.dtype)

    L = qf_ref.shape[0]
    d = qf_ref.shape[1] // heads
    r = lax.broadcasted_iota(jnp.int32, (L, L), 0)
    c = lax.broadcasted_iota(jnp.int32, (L, L), 1)
    lower, upper = c <= r, c >= r
    tri = {True: lower.astype(BF16), False: upper.astype(BF16)}

    for forward, qp_ref, out_ref in ((True, qpf_ref, hf_ref), (False, qpb_ref, hb_ref)):
        for h in range(heads):
            cols = slice(h * d, (h + 1) * d)
            ci = h if forward else heads + h
            out_ref[:, cols] = _mlstm_state(
                qp_ref[:, cols], in_scr.at[ci], u_scr.at[ci], r_scr.at[ci], ml_scr.at[ci],
                bl_scr.at[ci], c_scr.at[ci], m_scr.at[ci])

    for forward, q_ref, k_ref, v_ref, g_ref, gt_ref in (
            (True, qf_ref, kf_ref, vf_ref, gf_ref, gtf_ref),
            (False, qb_ref, kb_ref, vb_ref, gb_ref, gtb_ref)):
        gi = 0 if forward else 2
        g = g_ref[...] + brow_ref[...]
        gt = gt_ref[...] + bcol_ref[...]
        cum_col = sum(jnp.dot(tri[forward], part, preferred_element_type=F32)
                      for part in _split3(_log_sigmoid(g)))
        cum_row = sum(jnp.dot(part, tri[not forward], preferred_element_type=F32)
                      for part in _split3(_log_sigmoid(gt)))
        valid_t = upper if forward else lower
        for h in range(heads):
            ii, fi = h * GATES_PER_HEAD + gi, h * GATES_PER_HEAD + gi + 1
            cols = slice(h * d, (h + 1) * d)
            ci = h if forward else heads + h
            _mlstm_local(
                q_ref[:, cols], k_ref[:, cols], v_ref[cols, :],
                gt[ii:ii + 1, :], cum_row[fi:fi + 1, :], g[:, ii:ii + 1] - cum_col[:, fi:fi + 1],
                valid_t, forward, in_scr.at[ci], u_scr.at[ci], r_scr.at[ci], ml_scr.at[ci],
                bl_scr.at[ci])


def _mlstm(qk, avt, gates, gates_t, bias_row, bias_col, *, batch, seq, heads, head_dim):
    n = qk.shape[0]
    L = ML_CHUNK
    nc = seq // L
    width = heads * head_dim
    n_gates = gates_t.shape[0]
    chains = 2 * heads
    cur = lambda c: jnp.minimum(c, nc - 1)
    prev = lambda c: jnp.maximum(c - 1, 0)
    fw = lambda b, j: b * nc + j
    bw = lambda b, j: b * nc + (nc - 1 - j)

    def specs(rowf):
        return [
            pl.BlockSpec((L, width), lambda b, c: (rowf(b, cur(c)), 0)),
            pl.BlockSpec((L, width), lambda b, c: (rowf(b, cur(c)), 1)),
            pl.BlockSpec((width, L), lambda b, c: (0, rowf(b, cur(c)))),
            pl.BlockSpec((L, LANES), lambda b, c: (rowf(b, cur(c)), 0)),
            pl.BlockSpec((n_gates, L), lambda b, c: (0, rowf(b, cur(c)))),
            pl.BlockSpec((L, width), lambda b, c: (rowf(b, prev(c)), 0)),
        ]

    def call_args():
        return [qk, qk, avt, gates, gates_t, qk]

    return pl.pallas_call(
        functools.partial(_mlstm_body, heads=heads),
        grid=(batch, nc + 1),
        in_specs=specs(fw) + specs(bw) + [
            pl.BlockSpec((1, LANES), lambda b, c: (0, 0)),
            pl.BlockSpec((n_gates, 1), lambda b, c: (0, 0)),
        ],
        out_specs=[pl.BlockSpec((L, width), lambda b, c: (fw(b, prev(c)), 0)),
                   pl.BlockSpec((L, width), lambda b, c: (bw(b, prev(c)), 0))],
        out_shape=[jax.ShapeDtypeStruct((n, width), F32)] * 2,
        scratch_shapes=[
            pltpu.VMEM((chains, head_dim + ONES_ROWS, head_dim), F32),
            pltpu.VMEM((chains, 1, 1), F32),
            pltpu.VMEM((chains, head_dim + ONES_ROWS, L), F32),
            pltpu.VMEM((chains, head_dim + ONES_ROWS, head_dim), F32),
            pltpu.VMEM((chains, 2, L), F32),
            pltpu.VMEM((chains, 1, 1), F32),
            pltpu.VMEM((chains, 1, 1), F32),
        ],
        compiler_params=_params(2),
        name="mlstm",
    )(*call_args(), *call_args(), bias_row, bias_col)


def _attn_body(lam_ref, g_ref, q_ref, k_ref, vt_ref, bz_ref, o_ref, m_scr, acc_scr,
               s_scr0, s_scr1, p_scr0, p_scr1, a_scr0, a_scr1, x_scr0, x_scr1, *, lambda_init):
    tq = q_ref.shape[0]
    n_chunks, d_v, tk = vt_ref.shape
    q = q_ref[...]
    lane = lax.broadcasted_iota(jnp.int32, (tq, LANES), 1)
    zero = jnp.zeros_like(q)
    first = (lane % (LANES // 2)) < LANES // 4
    q_maps = (jnp.where(first, q, zero), jnp.where(first, zero, q))
    s_scr, p_scr, a_scr = (s_scr0, s_scr1), (p_scr0, p_scr1), (a_scr0, a_scr1)
    x_scr = (x_scr0, x_scr1)
    ones_rows = jnp.ones((acc_scr.shape[1] - d_v, tk), BF16)

    m_scr[...] = jnp.full(m_scr.shape, -jnp.inf, F32)
    acc_scr[...] = jnp.zeros(acc_scr.shape, F32)

    def scores(j, slot):
        k = k_ref[pl.ds(pl.multiple_of(j * tk, tk), tk), :]
        for mi in range(2):
            s = lax.dot_general(k, q_maps[mi], _NT, preferred_element_type=F32)
            s_scr[slot][mi] = s
            x_scr[slot][mi] = jnp.max(s, axis=0, keepdims=True)

    def softmax(slot):
        for mi in range(2):
            m_prev = m_scr[mi]
            m_cur = jnp.maximum(m_prev, x_scr[slot][mi])
            m_scr[mi] = m_cur
            a_scr[slot][mi] = jnp.exp2(m_prev - m_cur)
            p_scr[slot][mi] = jnp.exp2(s_scr[slot][mi] - m_cur).astype(BF16)

    def values(j, slot):
        vt = jnp.concatenate([vt_ref[j], ones_rows], axis=0)
        for mi in range(2):
            acc_scr[mi] = a_scr[slot][mi] * acc_scr[mi] + jnp.dot(
                vt, p_scr[slot][mi], preferred_element_type=F32)

    def two_steps(t):
        j = 2 * t + 1
        scores(j + 1, 0)
        softmax(1)
        values(j - 1, 0)
        scores(j + 2, 1)
        softmax(0)
        values(j, 1)

    scores(0, 0)
    scores(1, 1)
    softmax(0)
    n_pairs = (n_chunks - 2) // 2
    unroll = min(ATTN_UNROLL_PAIRS, max(n_pairs, 1))
    head = n_pairs % unroll
    for t in range(head):
        two_steps(t)

    def loop_body(i, carry):
        for u in range(unroll):
            two_steps(head + i * unroll + u)
        return carry

    lax.fori_loop(0, n_pairs // unroll, loop_body, 0)
    softmax(1)
    values(n_chunks - 2, 0)
    values(n_chunks - 1, 1)

    lp = lam_ref[...]
    dots = jnp.sum(lp[0:1, :] * lp[1:2, :], axis=-1, keepdims=True), \
        jnp.sum(lp[2:3, :] * lp[3:4, :], axis=-1, keepdims=True)
    lam = jnp.exp(dots[0]) - jnp.exp(dots[1]) + lambda_init
    acc0, acc1 = acc_scr[0], acc_scr[1]
    att = (acc0[:d_v] / acc0[d_v:d_v + 1] - lam * (acc1[:d_v] / acc1[d_v:d_v + 1]))
    ms = jnp.mean(att * att, axis=0, keepdims=True)
    att = (att * lax.rsqrt(ms + NORM_EPS)).T
    att = att * g_ref[...] * (1.0 - lambda_init)
    o_ref[...] = (att * _silu(bz_ref[...].astype(F32))).astype(BF16)


def _diff_attn(lam_p, subln_g, vt, p, *, batch, seq, heads, q_col, k_col, z_col, tq, lambda_init):
    n = p.shape[0]
    nq = seq // tq
    tk = vt.shape[2]
    nk = seq // tk
    body = functools.partial(_attn_body, lambda_init=lambda_init)
    return pl.pallas_call(
        body,
        grid=(batch, heads, nq),
        in_specs=[
            pl.BlockSpec(lam_p.shape, lambda b, h, i: (0, 0)),
            pl.BlockSpec((1, LANES), lambda b, h, i: (0, 0)),
            pl.BlockSpec((tq, LANES), lambda b, h, i: (b * nq + i, q_col + h)),
            pl.BlockSpec((seq, LANES), lambda b, h, i: (b, k_col + h)),
            pl.BlockSpec((nk, LANES, tk), lambda b, h, i: (b, h, 0)),
            pl.BlockSpec((tq, LANES), lambda b, h, i: (b * nq + i, z_col + h)),
        ],
        out_specs=pl.BlockSpec((tq, LANES), lambda b, h, i: (b * nq + i, h)),
        out_shape=jax.ShapeDtypeStruct((n, heads * LANES), BF16),
        scratch_shapes=[
            pltpu.VMEM((2, 1, tq), F32),
            pltpu.VMEM((2, LANES + ONES_ROWS, tq), F32),
            pltpu.VMEM((2, tk, tq), F32), pltpu.VMEM((2, tk, tq), F32),
            pltpu.VMEM((2, tk, tq), BF16), pltpu.VMEM((2, tk, tq), BF16),
            pltpu.VMEM((2, 1, tq), F32), pltpu.VMEM((2, 1, tq), F32),
            pltpu.VMEM((2, 1, tq), F32), pltpu.VMEM((2, 1, tq), F32),
        ],
        compiler_params=_params(3),
        name="diff_attn",
    )(lam_p, subln_g, p, p, vt, p)


def _merge_body(hf_ref, hb_ref, ao_ref, az_ref, yb_ref, ga_ref, gb_ref, x_ref, mlg_ref, gbias_ref,
                fing_ref, wa_ref, wb_ref, wo_ref, o_ref, *, head_dim, final_norm):
    hs = hf_ref[...] + hb_ref[...]
    width = hs.shape[1]
    parts = []
    for c0 in range(0, width, head_dim):
        hh = hs[:, c0:c0 + head_dim]
        ms = jnp.mean(hh * hh, axis=-1, keepdims=True)
        parts.append(hh * lax.rsqrt(ms + NORM_EPS))
    hm = jnp.concatenate(parts, axis=-1) * mlg_ref[...]
    y_a = hm * _sigmoid(ao_ref[...].astype(F32)) * _silu(az_ref[...].astype(F32))
    pa = jnp.dot(y_a.astype(BF16), wa_ref[...], preferred_element_type=F32)
    pb = jnp.dot(yb_ref[...], wb_ref[...], preferred_element_type=F32)
    gbias = gbias_ref[...]
    mix = (_sigmoid(ga_ref[...].astype(F32) + gbias[0:1, :]) * pa
           + _sigmoid(gb_ref[...].astype(F32) + gbias[1:2, :]) * pb)
    xn = x_ref[...] + jnp.dot(mix.astype(BF16), wo_ref[...], preferred_element_type=F32)
    if final_norm:
        ms = jnp.mean(xn * xn, axis=-1, keepdims=True)
        xn = xn * lax.rsqrt(ms + NORM_EPS) * fing_ref[...]
    o_ref[...] = xn


def _merge(h_fw, h_bw, p, y_b, x2, ml_g, gate_bias, final_g, w_a, w_b, w_o, *, tm, head_dim,
           o_col, z_col, ga_col, gb_col, final_norm):
    n, d = x2.shape
    row = lambda c: pl.BlockSpec((tm, d), lambda i: (i, c))
    const = lambda shape: pl.BlockSpec(shape, lambda i: (0, 0))
    body = functools.partial(_merge_body, head_dim=head_dim, final_norm=final_norm)
    return pl.pallas_call(
        body,
        grid=(n // tm,),
        in_specs=[row(0), row(0), row(o_col), row(z_col), row(0), row(ga_col), row(gb_col), row(0),
                  const((1, d)), const((2, d)), const((1, d)),
                  const((d, d)), const((d, d)), const((d, d))],
        out_specs=row(0),
        out_shape=jax.ShapeDtypeStruct((n, d), F32),
        compiler_params=_params(1),
        name="merge",
    )(h_fw, h_bw, p, p, y_b, p, p, x2, ml_g, gate_bias, final_g, w_a, w_b, w_o)


def kernel(x, positions, norm_g, w_in, ml_gate_b, ml_conv_w, ml_norm_g, da_lambda, da_subln_g,
           gate_b, w_branch_a, w_branch_b, w_out, final_g):
    batch, seq, d = x.shape
    depth = w_in.shape[0]
    ml_heads, ml_head_dim = ml_norm_g.shape[1], ml_norm_g.shape[2]
    ml_width = ml_heads * ml_head_dim
    n_gates = GATES_PER_HEAD * ml_heads
    da_v_dim = da_subln_g.shape[1]
    da_qk_dim = da_lambda.shape[2]
    da_width = w_branch_b.shape[1]
    da_heads = da_width // da_v_dim
    assert ml_width == d and da_width == d and da_v_dim == LANES and 2 * da_qk_dim == LANES
    assert w_in.shape[2] == 11 * d + n_gates and ml_gate_b.shape[1] == GATES_PER_HEAD
    n = batch * seq

    col = {name: i for i, name in enumerate(
        ("a_q", "a_k", "a_o", "a_z", "b_q", "b_k", "b_z", "g_a", "g_b"))}
    bv0 = 7 * d + n_gates

    tiles = _tiles(seq)
    tm_proj, tm_prep, tm_merge = tiles["in_proj"], tiles["prep"], tiles["merge"]
    tq, tk = tiles["attn_q"], tiles["attn_k"]

    x2 = x.reshape(n, d)
    pos2 = positions.reshape(n, 1)
    inv_freq = ROPE_THETA ** (-jnp.arange(0, da_qk_dim, 2, dtype=F32) / da_qk_dim)
    freq_tile = jnp.tile(inv_freq, LANES // inv_freq.shape[0]).reshape(1, LANES)

    for layer in range(depth):
        w = w_in[layer]
        w_qk = w[:, 5 * d + n_gates:bv0].reshape(d, 2 * da_heads, 2, 2, da_qk_dim // 2)
        w_qk = w_qk.transpose(0, 1, 3, 2, 4).reshape(d, 2 * da_width)
        w_main = jnp.concatenate([w[:, :2 * d], w[:, 3 * d:5 * d], w_qk, w[:, bv0 + d:]],
                                 axis=1).astype(BF16)
        wvt = w[:, bv0:bv0 + d].astype(BF16)
        wavt = w[:, 2 * d:3 * d].astype(BF16)
        w_g = w[:, 5 * d:5 * d + n_gates].reshape(d, GATES_PER_HEAD, ml_heads).transpose(0, 2, 1)
        w_g = w_g.reshape(d, n_gates).astype(BF16)
        wg = jnp.pad(w_g, ((0, 0), (0, LANES - n_gates)))
        bias_hg = ml_gate_b[layer].astype(F32).T
        bias_row = jnp.pad(bias_hg.reshape(1, n_gates), ((0, 0), (0, LANES - n_gates)))
        bias_col = bias_hg.reshape(n_gates, 1)

        p, gates, gates_t, vt, avt = _in_proj(
            x2, norm_g[layer].reshape(1, d), w_main, wg, w_g.T, wvt, wavt, pos2, freq_tile,
            tm=tm_proj, tn=d, tk=tk, rope_cols=(col["b_q"], col["b_k"]),
            q_scale=da_qk_dim ** -0.5 * math.log2(math.e))
        qk = _prep(p, ml_conv_w[layer].astype(F32), seq=seq, ml_width=ml_width, tm=tm_prep,
                   k_scale=ml_head_dim ** -0.5)
        h_fw, h_bw = _mlstm(qk, avt, gates, gates_t, bias_row, bias_col,
                            batch=batch, seq=seq, heads=ml_heads, head_dim=ml_head_dim)
        lambda_init = 0.8 - 0.6 * math.exp(-0.3 * layer)
        y_b = _diff_attn(da_lambda[layer].astype(F32), da_subln_g[layer].reshape(1, LANES), vt, p,
                         batch=batch, seq=seq, heads=da_heads, q_col=col["b_q"] * da_heads,
                         k_col=col["b_k"] * da_heads, z_col=col["b_z"] * da_heads, tq=tq,
                         lambda_init=lambda_init)
        x2 = _merge(h_fw, h_bw, p, y_b, x2, ml_norm_g[layer].reshape(1, ml_width), gate_b[layer],
                    final_g.reshape(1, d), w_branch_a[layer].astype(BF16),
                    w_branch_b[layer].astype(BF16), w_out[layer].astype(BF16), tm=tm_merge,
                    head_dim=ml_head_dim, o_col=col["a_o"], z_col=col["a_z"], ga_col=col["g_a"],
                    gb_col=col["g_b"], final_norm=(layer == depth - 1))
    return x2.reshape(batch, seq, d)
```

```python
import functools
import math

import jax
import jax.numpy as jnp
from jax import lax
from jax.experimental import pallas as pl
from jax.experimental.pallas import tpu as pltpu

F32 = jnp.float32
BF16 = jnp.bfloat16

NORM_EPS = 1e-6
ROPE_THETA = 10000.0
GATES_PER_HEAD = 4
LANES = 128
SUBLANES = 8
ONES_ROWS = 16
MXU_WIDTH = 256
V7X_VMEM_BYTES = 64 * 1024 * 1024
VMEM_LIMIT_BYTES = V7X_VMEM_BYTES - 8 * 1024 * 1024

ML_CHUNK = MXU_WIDTH
CONV_ROWS = 128
CONV_SLAB_LANES = 256
ATTN_UNROLL_PAIRS = 2

_NT = (((1,), (1,)), ((), ()))
_TNT = (((0,), (1,)), ((), ()))


def _tiles(seq):
    t = dict(in_proj=min(1024, seq), prep=min(512, seq), attn_q=min(2048, seq),
             attn_k=min(512, seq), merge=min(512, seq))
    assert all(seq % v == 0 for v in t.values()) and seq % ML_CHUNK == 0
    assert (seq // t["attn_k"]) % 2 == 0 and t["in_proj"] % t["attn_k"] == 0
    return t


def _params(n_axes):
    return pltpu.CompilerParams(
        dimension_semantics=("arbitrary",) * n_axes,
        vmem_limit_bytes=VMEM_LIMIT_BYTES)


def _sigmoid(x):
    return 1.0 / (1.0 + jnp.exp2(x * (-math.log2(math.e))))


def _silu(x):
    return x * _sigmoid(x)


def _log_sigmoid(x):
    return jnp.minimum(x, 0.0) - jnp.log1p(jnp.exp(-jnp.abs(x)))


def _rope_tables(pos_ref, freq_ref):
    tm = pos_ref.shape[0]
    d_map = LANES // 2
    n_freq = d_map // 2
    groups = LANES // n_freq
    rows_g = tm // groups
    lane_c = lax.broadcasted_iota(jnp.int32, (rows_g, LANES), 1)
    pos_c = pos_ref[0:rows_g, :]
    for g in range(1, groups):
        pos_c = jnp.where(lane_c >= g * n_freq, pos_ref[g * rows_g:(g + 1) * rows_g, :], pos_c)
    ang_c = pos_c.astype(F32) * freq_ref[...]

    def spread(table_c):
        parts = []
        for g in range(groups):
            base = table_c if g == 0 else pltpu.roll(table_c, LANES - g * n_freq, 1)
            full = base
            for r in range(1, groups):
                full = jnp.where(lane_c >= r * n_freq, pltpu.roll(base, r * n_freq, 1), full)
            parts.append(full)
        return jnp.concatenate(parts, axis=0)

    lane = lax.broadcasted_iota(jnp.int32, (tm, LANES), 1)
    sin = spread(jnp.sin(ang_c))
    return spread(jnp.cos(ang_c)), jnp.where(lane < LANES // 2, -sin, sin)


def _in_proj_body(x_ref, g_ref, w_ref, wg_ref, wgt_ref, wvt_ref, wavt_ref, pos_ref, freq_ref,
                  p_ref, gates_ref, gatest_ref, vt_ref, avt_ref, h_scr, cos_scr, sin_scr,
                  *, rope_cols, q_scale):
    j = pl.program_id(1)

    @pl.when(j == 0)
    def _():
        cos_scr[...], sin_scr[...] = _rope_tables(pos_ref, freq_ref)
        x = x_ref[...]
        ms = jnp.mean(x * x, axis=-1, keepdims=True)
        h = (x * lax.rsqrt(ms + NORM_EPS) * g_ref[...]).astype(BF16)
        h_scr[...] = h
        gates_ref[...] = jnp.dot(h, wg_ref[...], preferred_element_type=F32)
        gatest_ref[...] = lax.dot_general(wgt_ref[...], h, _NT, preferred_element_type=F32)
        avt_ref[...] = lax.dot_general(wavt_ref[...], h, _TNT,
                                       preferred_element_type=F32).astype(BF16)
        vt = lax.dot_general(wvt_ref[...], h, _TNT, preferred_element_type=F32).astype(BF16)
        tk = vt_ref.shape[2]
        for c in range(vt_ref.shape[0]):
            vt_ref[c] = vt[:, c * tk:(c + 1) * tk]

    is_rope = functools.reduce(jnp.logical_or, [j == c for c in rope_cols])

    @pl.when(jnp.logical_not(is_rope))
    def _():
        p_ref[...] = jnp.dot(h_scr[...], w_ref[...], preferred_element_type=F32).astype(BF16)

    @pl.when(is_rope)
    def _():
        r = jnp.dot(h_scr[...], w_ref[...], preferred_element_type=F32)
        scale = jnp.where(j == rope_cols[0], q_scale, 1.0)
        cos = cos_scr[...] * scale
        sin = sin_scr[...] * scale
        for c0 in range(0, r.shape[1], LANES):
            t = r[:, c0:c0 + LANES]
            p_ref[:, c0:c0 + LANES] = (t * cos + pltpu.roll(t, LANES // 2, 1) * sin).astype(BF16)


def _in_proj(x2, norm_g, w_main, wg, wgt, wvt, wavt, pos2, freq_tile, *, tm, tn, tk, rope_cols,
             q_scale):
    n, d = x2.shape
    width = w_main.shape[1]
    ng = wgt.shape[0]
    dv = wvt.shape[1]
    dav = wavt.shape[1]
    return pl.pallas_call(
        functools.partial(_in_proj_body, rope_cols=rope_cols, q_scale=q_scale),
        grid=(n // tm, width // tn),
        in_specs=[
            pl.BlockSpec((tm, d), lambda i, j: (i, 0)),
            pl.BlockSpec((1, d), lambda i, j: (0, 0)),
            pl.BlockSpec((d, tn), lambda i, j: (0, j)),
            pl.BlockSpec((d, LANES), lambda i, j: (0, 0)),
            pl.BlockSpec((ng, d), lambda i, j: (0, 0)),
            pl.BlockSpec((d, dv), lambda i, j: (0, 0)),
            pl.BlockSpec((d, dav), lambda i, j: (0, 0)),
            pl.BlockSpec((tm, 1), lambda i, j: (i, 0)),
            pl.BlockSpec((1, LANES), lambda i, j: (0, 0)),
        ],
        out_specs=[
            pl.BlockSpec((tm, tn), lambda i, j: (i, j)),
            pl.BlockSpec((tm, LANES), lambda i, j: (i, 0)),
            pl.BlockSpec((ng, tm), lambda i, j: (0, i)),
            pl.BlockSpec((tm // tk, dv, tk), lambda i, j: (i, 0, 0)),
            pl.BlockSpec((dav, tm), lambda i, j: (0, i)),
        ],
        out_shape=[
            jax.ShapeDtypeStruct((n, width), BF16),
            jax.ShapeDtypeStruct((n, LANES), F32),
            jax.ShapeDtypeStruct((ng, n), F32),
            jax.ShapeDtypeStruct((n // tk, dv, tk), BF16),
            jax.ShapeDtypeStruct((dav, n), BF16),
        ],
        scratch_shapes=[pltpu.VMEM((tm, d), BF16), pltpu.VMEM((tm, LANES), F32),
                        pltpu.VMEM((tm, LANES), F32)],
        compiler_params=_params(2),
        name="in_proj",
    )(x2, norm_g, w_main, wg, wgt, wvt, wavt, pos2, freq_tile)


def _conv_shift_matrix(conv_width):
    halo, pad, win = SUBLANES, conv_width // 2, CONV_ROWS + 2 * SUBLANES
    k_pad = -(-conv_width * win // MXU_WIDTH) * MXU_WIDTH
    r = jnp.arange(CONV_ROWS)[:, None]
    c = jnp.arange(k_pad)[None, :]
    hit = functools.reduce(jnp.logical_or,
                           [c == t * win + r + t + halo - pad for t in range(conv_width)])
    return hit.astype(BF16)


def _prep_body(qk_ref, prev_ref, next_ref, cw_ref, shift_ref, qk_out, ext_scr, *, tiles_per_seq,
               conv_width, k_scale):
    i = pl.program_id(0)
    tm = qk_ref.shape[0]
    width = qk_ref.shape[1]
    halo = SUBLANES
    win = CONV_ROWS + 2 * halo
    t_in_seq = i % tiles_per_seq
    has_prev = (t_in_seq > 0).astype(F32)
    has_next = (t_in_seq < tiles_per_seq - 1).astype(F32)
    shift = shift_ref[...]
    slab = ext_scr.shape[1]
    k_fill = jnp.zeros((shift.shape[1] - conv_width * win, slab), BF16)

    for c0 in range(0, width, slab):
        cols = slice(c0, c0 + slab)
        ext_scr[0:halo, :] = prev_ref[:, cols].astype(F32) * has_prev
        ext_scr[halo:halo + tm, :] = qk_ref[:, cols].astype(F32)
        ext_scr[halo + tm:halo + tm + halo, :] = next_ref[:, cols].astype(F32) * has_next
        scale = 1.0 if c0 < width // 2 else k_scale
        taps = [cw_ref[t:t + 1, cols] for t in range(conv_width)]
        for r0 in range(0, tm, CONV_ROWS):
            window = ext_scr[r0:r0 + win, :]
            prods = jnp.concatenate([(window * tap).astype(BF16) for tap in taps] + [k_fill], axis=0)
            acc = jnp.dot(shift, prods, preferred_element_type=F32)
            qk_out[r0:r0 + CONV_ROWS, cols] = (_silu(acc) * scale).astype(BF16)


def _prep(p, conv_w, *, seq, ml_width, tm, k_scale):
    n = p.shape[0]
    n_tiles = n // tm
    halo_blocks = n // SUBLANES
    qk_w = 2 * ml_width
    assert conv_w.shape[0] // 2 <= SUBLANES
    shift = _conv_shift_matrix(conv_w.shape[0])
    body = functools.partial(
        _prep_body, tiles_per_seq=seq // tm, conv_width=conv_w.shape[0], k_scale=k_scale)
    per_tile = tm // SUBLANES
    return pl.pallas_call(
        body,
        grid=(n_tiles,),
        in_specs=[
            pl.BlockSpec((tm, qk_w), lambda i: (i, 0)),
            pl.BlockSpec((SUBLANES, qk_w), lambda i: (jnp.maximum(i * per_tile - 1, 0), 0)),
            pl.BlockSpec((SUBLANES, qk_w),
                         lambda i: (jnp.minimum((i + 1) * per_tile, halo_blocks - 1), 0)),
            pl.BlockSpec((conv_w.shape[0], qk_w), lambda i: (0, 0)),
            pl.BlockSpec(shift.shape, lambda i: (0, 0)),
        ],
        out_specs=pl.BlockSpec((tm, qk_w), lambda i: (i, 0)),
        out_shape=jax.ShapeDtypeStruct((n, qk_w), BF16),
        scratch_shapes=[pltpu.VMEM((tm + 2 * SUBLANES, CONV_SLAB_LANES), F32)],
        compiler_params=_params(1),
        name="prep",
    )(p, p, p, conv_w, shift)


def _split3(f):
    hi = f.astype(BF16)
    r1 = f - hi.astype(F32)
    mid = r1.astype(BF16)
    lo = (r1 - mid.astype(F32)).astype(BF16)
    return hi, mid, lo


def _mlstm_chunk(qc, kc, vt, i_row, b_row, a_col, valid_t, c_ref, m_ref, forward):
    L = qc.shape[0]
    d = vt.shape[0]
    a_row = i_row - b_row
    m_st = m_ref[...]
    log_d = jnp.where(valid_t, a_col + b_row, -jnp.inf)
    m_t = jnp.maximum(b_row + m_st, jnp.max(log_d, axis=0, keepdims=True))
    w_intra = lax.dot_general(kc, qc, _NT, preferred_element_type=F32) * jnp.exp(log_d - m_t)
    s_inter = jnp.exp(b_row + m_st - m_t)
    c_aug = c_ref[...]
    inter = lax.dot_general(c_aug.astype(BF16), qc, _NT, preferred_element_type=F32)
    vt_aug = jnp.concatenate([vt, jnp.ones((ONES_ROWS, L), BF16)], axis=0)
    intra = jnp.dot(vt_aug, w_intra.astype(BF16), preferred_element_type=F32)
    num = intra[:d] + s_inter * inter[:d]
    den = intra[d:d + 1] + s_inter * inter[d:d + 1]
    h_t = num / jnp.maximum(jnp.abs(den), jnp.exp(-m_t))

    b_last = b_row[:, L - 1:L] if forward else b_row[:, 0:1]
    m_new = jnp.maximum(b_last + m_st, jnp.max(b_last + a_row, axis=-1, keepdims=True))
    w_row = jnp.exp(b_last + a_row - m_new)
    decay = jnp.exp(b_last + m_st - m_new)
    w_b = w_row.astype(BF16)
    lhs = jnp.concatenate([vt * w_b, jnp.broadcast_to(w_b, (ONES_ROWS, L))], axis=0)
    c_ref[...] = decay * c_aug + jnp.dot(lhs, kc, preferred_element_type=F32)
    m_ref[...] = m_new
    return h_t.T


def _mlstm_body(qf_ref, kf_ref, vf_ref, gf_ref, gtf_ref, qb_ref, kb_ref, vb_ref, gb_ref, gtb_ref,
                brow_ref, bcol_ref, hf_ref, hb_ref, c_scr, m_scr, *, heads):
    @pl.when(pl.program_id(1) == 0)
    def _():
        for ref in (c_scr, m_scr):
            ref[...] = jnp.zeros(ref.shape, ref.dtype)

    L = qf_ref.shape[0]
    d = qf_ref.shape[1] // heads
    r = lax.broadcasted_iota(jnp.int32, (L, L), 0)
    c = lax.broadcasted_iota(jnp.int32, (L, L), 1)
    lower, upper = c <= r, c >= r
    tri = {True: lower.astype(BF16), False: upper.astype(BF16)}

    for forward, q_ref, k_ref, v_ref, g_ref, gt_ref, out_ref in (
            (True, qf_ref, kf_ref, vf_ref, gf_ref, gtf_ref, hf_ref),
            (False, qb_ref, kb_ref, vb_ref, gb_ref, gtb_ref, hb_ref)):
        gi = 0 if forward else 2
        g = g_ref[...] + brow_ref[...]
        gt = gt_ref[...] + bcol_ref[...]
        cum_col = sum(jnp.dot(tri[forward], part, preferred_element_type=F32)
                      for part in _split3(_log_sigmoid(g)))
        cum_row = sum(jnp.dot(part, tri[not forward], preferred_element_type=F32)
                      for part in _split3(_log_sigmoid(gt)))
        valid_t = upper if forward else lower
        for h in range(heads):
            ii, fi = h * GATES_PER_HEAD + gi, h * GATES_PER_HEAD + gi + 1
            cols = slice(h * d, (h + 1) * d)
            ci = h if forward else heads + h
            out_ref[:, cols] = _mlstm_chunk(
                q_ref[:, cols], k_ref[:, cols], v_ref[cols, :],
                gt[ii:ii + 1, :], cum_row[fi:fi + 1, :], g[:, ii:ii + 1] - cum_col[:, fi:fi + 1],
                valid_t, c_scr.at[ci], m_scr.at[ci], forward)


def _mlstm(qk, avt, gates, gates_t, bias_row, bias_col, *, batch, seq, heads, head_dim):
    n = qk.shape[0]
    L = ML_CHUNK
    nc = seq // L
    width = heads * head_dim
    n_gates = gates_t.shape[0]
    fw = lambda b, c: b * nc + c
    bw = lambda b, c: b * nc + (nc - 1 - c)

    def specs(rowf):
        return [
            pl.BlockSpec((L, width), lambda b, c: (rowf(b, c), 0)),
            pl.BlockSpec((L, width), lambda b, c: (rowf(b, c), 1)),
            pl.BlockSpec((width, L), lambda b, c: (0, rowf(b, c))),
            pl.BlockSpec((L, LANES), lambda b, c: (rowf(b, c), 0)),
            pl.BlockSpec((n_gates, L), lambda b, c: (0, rowf(b, c))),
        ]

    def call_args():
        return [qk, qk, avt, gates, gates_t]

    return pl.pallas_call(
        functools.partial(_mlstm_body, heads=heads),
        grid=(batch, nc),
        in_specs=specs(fw) + specs(bw) + [
            pl.BlockSpec((1, LANES), lambda b, c: (0, 0)),
            pl.BlockSpec((n_gates, 1), lambda b, c: (0, 0)),
        ],
        out_specs=[pl.BlockSpec((L, width), lambda b, c: (fw(b, c), 0)),
                   pl.BlockSpec((L, width), lambda b, c: (bw(b, c), 0))],
        out_shape=[jax.ShapeDtypeStruct((n, width), F32)] * 2,
        scratch_shapes=[
            pltpu.VMEM((2 * heads, head_dim + ONES_ROWS, head_dim), F32),
            pltpu.VMEM((2 * heads, 1, 1), F32),
        ],
        compiler_params=_params(2),
        name="mlstm",
    )(*call_args(), *call_args(), bias_row, bias_col)


def _attn_body(lam_ref, g_ref, q_ref, k_ref, vt_ref, bz_ref, o_ref, m_scr, acc_scr,
               s_scr0, s_scr1, p_scr0, p_scr1, a_scr0, a_scr1, x_scr0, x_scr1, *, lambda_init):
    tq = q_ref.shape[0]
    n_chunks, d_v, tk = vt_ref.shape
    q = q_ref[...]
    lane = lax.broadcasted_iota(jnp.int32, (tq, LANES), 1)
    zero = jnp.zeros_like(q)
    first = (lane % (LANES // 2)) < LANES // 4
    q_maps = (jnp.where(first, q, zero), jnp.where(first, zero, q))
    s_scr, p_scr, a_scr = (s_scr0, s_scr1), (p_scr0, p_scr1), (a_scr0, a_scr1)
    x_scr = (x_scr0, x_scr1)
    ones_rows = jnp.ones((acc_scr.shape[1] - d_v, tk), BF16)

    m_scr[...] = jnp.full(m_scr.shape, -jnp.inf, F32)
    acc_scr[...] = jnp.zeros(acc_scr.shape, F32)

    def scores(j, slot):
        k = k_ref[pl.ds(pl.multiple_of(j * tk, tk), tk), :]
        for mi in range(2):
            s = lax.dot_general(k, q_maps[mi], _NT, preferred_element_type=F32)
            s_scr[slot][mi] = s
            x_scr[slot][mi] = jnp.max(s, axis=0, keepdims=True)

    def softmax(slot):
        for mi in range(2):
            m_prev = m_scr[mi]
            m_cur = jnp.maximum(m_prev, x_scr[slot][mi])
            m_scr[mi] = m_cur
            a_scr[slot][mi] = jnp.exp2(m_prev - m_cur)
            p_scr[slot][mi] = jnp.exp2(s_scr[slot][mi] - m_cur).astype(BF16)

    def values(j, slot):
        vt = jnp.concatenate([vt_ref[j], ones_rows], axis=0)
        for mi in range(2):
            acc_scr[mi] = a_scr[slot][mi] * acc_scr[mi] + jnp.dot(
                vt, p_scr[slot][mi], preferred_element_type=F32)

    def two_steps(t):
        j = 2 * t + 1
        scores(j + 1, 0)
        softmax(1)
        values(j - 1, 0)
        scores(j + 2, 1)
        softmax(0)
        values(j, 1)

    scores(0, 0)
    scores(1, 1)
    softmax(0)
    n_pairs = (n_chunks - 2) // 2
    unroll = min(ATTN_UNROLL_PAIRS, max(n_pairs, 1))
    head = n_pairs % unroll
    for t in range(head):
        two_steps(t)

    def loop_body(i, carry):
        for u in range(unroll):
            two_steps(head + i * unroll + u)
        return carry

    lax.fori_loop(0, n_pairs // unroll, loop_body, 0)
    softmax(1)
    values(n_chunks - 2, 0)
    values(n_chunks - 1, 1)

    lp = lam_ref[...]
    dots = jnp.sum(lp[0:1, :] * lp[1:2, :], axis=-1, keepdims=True), \
        jnp.sum(lp[2:3, :] * lp[3:4, :], axis=-1, keepdims=True)
    lam = jnp.exp(dots[0]) - jnp.exp(dots[1]) + lambda_init
    acc0, acc1 = acc_scr[0], acc_scr[1]
    att = (acc0[:d_v] / acc0[d_v:d_v + 1] - lam * (acc1[:d_v] / acc1[d_v:d_v + 1]))
    ms = jnp.mean(att * att, axis=0, keepdims=True)
    att = (att * lax.rsqrt(ms + NORM_EPS)).T
    att = att * g_ref[...] * (1.0 - lambda_init)
    o_ref[...] = (att * _silu(bz_ref[...].astype(F32))).astype(BF16)


def _diff_attn(lam_p, subln_g, vt, p, *, batch, seq, heads, q_col, k_col, z_col, tq, lambda_init):
    n = p.shape[0]
    nq = seq // tq
    tk = vt.shape[2]
    nk = seq // tk
    body = functools.partial(_attn_body, lambda_init=lambda_init)
    return pl.pallas_call(
        body,
        grid=(batch, heads, nq),
        in_specs=[
            pl.BlockSpec(lam_p.shape, lambda b, h, i: (0, 0)),
            pl.BlockSpec((1, LANES), lambda b, h, i: (0, 0)),
            pl.BlockSpec((tq, LANES), lambda b, h, i: (b * nq + i, q_col + h)),
            pl.BlockSpec((seq, LANES), lambda b, h, i: (b, k_col + h)),
            pl.BlockSpec((nk, LANES, tk), lambda b, h, i: (b, h, 0)),
            pl.BlockSpec((tq, LANES), lambda b, h, i: (b * nq + i, z_col + h)),
        ],
        out_specs=pl.BlockSpec((tq, LANES), lambda b, h, i: (b * nq + i, h)),
        out_shape=jax.ShapeDtypeStruct((n, heads * LANES), BF16),
        scratch_shapes=[
            pltpu.VMEM((2, 1, tq), F32),
            pltpu.VMEM((2, LANES + ONES_ROWS, tq), F32),
            pltpu.VMEM((2, tk, tq), F32), pltpu.VMEM((2, tk, tq), F32),
            pltpu.VMEM((2, tk, tq), BF16), pltpu.VMEM((2, tk, tq), BF16),
            pltpu.VMEM((2, 1, tq), F32), pltpu.VMEM((2, 1, tq), F32),
            pltpu.VMEM((2, 1, tq), F32), pltpu.VMEM((2, 1, tq), F32),
        ],
        compiler_params=_params(3),
        name="diff_attn",
    )(lam_p, subln_g, p, p, vt, p)


def _merge_body(hf_ref, hb_ref, ao_ref, az_ref, yb_ref, ga_ref, gb_ref, x_ref, mlg_ref, gbias_ref,
                fing_ref, wa_ref, wb_ref, wo_ref, o_ref, *, head_dim, final_norm):
    pa = None
    for c0 in range(0, hf_ref.shape[1], head_dim):
        cols = slice(c0, c0 + head_dim)
        hh = hf_ref[:, cols] + hb_ref[:, cols]
        ms = jnp.mean(hh * hh, axis=-1, keepdims=True)
        y_h = (hh * lax.rsqrt(ms + NORM_EPS) * mlg_ref[:, cols]
               * _sigmoid(ao_ref[:, cols].astype(F32)) * _silu(az_ref[:, cols].astype(F32)))
        term = jnp.dot(y_h.astype(BF16), wa_ref[cols, :], preferred_element_type=F32)
        pa = term if pa is None else pa + term
    pb = jnp.dot(yb_ref[...], wb_ref[...], preferred_element_type=F32)
    gbias = gbias_ref[...]
    mix = (_sigmoid(ga_ref[...].astype(F32) + gbias[0:1, :]) * pa
           + _sigmoid(gb_ref[...].astype(F32) + gbias[1:2, :]) * pb)
    xn = x_ref[...] + jnp.dot(mix.astype(BF16), wo_ref[...], preferred_element_type=F32)
    if final_norm:
        ms = jnp.mean(xn * xn, axis=-1, keepdims=True)
        xn = xn * lax.rsqrt(ms + NORM_EPS) * fing_ref[...]
    o_ref[...] = xn


def _merge(h_fw, h_bw, p, y_b, x2, ml_g, gate_bias, final_g, w_a, w_b, w_o, *, tm, head_dim,
           o_col, z_col, ga_col, gb_col, final_norm):
    n, d = x2.shape
    row = lambda c: pl.BlockSpec((tm, d), lambda i: (i, c))
    const = lambda shape: pl.BlockSpec(shape, lambda i: (0, 0))
    body = functools.partial(_merge_body, head_dim=head_dim, final_norm=final_norm)
    return pl.pallas_call(
        body,
        grid=(n // tm,),
        in_specs=[row(0), row(0), row(o_col), row(z_col), row(0), row(ga_col), row(gb_col), row(0),
                  const((1, d)), const((2, d)), const((1, d)),
                  const((d, d)), const((d, d)), const((d, d))],
        out_specs=row(0),
        out_shape=jax.ShapeDtypeStruct((n, d), F32),
        compiler_params=_params(1),
        name="merge",
    )(h_fw, h_bw, p, p, y_b, p, p, x2, ml_g, gate_bias, final_g, w_a, w_b, w_o)


def kernel(x, positions, norm_g, w_in, ml_gate_b, ml_conv_w, ml_norm_g, da_lambda, da_subln_g,
           gate_b, w_branch_a, w_branch_b, w_out, final_g):
    batch, seq, d = x.shape
    depth = w_in.shape[0]
    ml_heads, ml_head_dim = ml_norm_g.shape[1], ml_norm_g.shape[2]
    ml_width = ml_heads * ml_head_dim
    n_gates = GATES_PER_HEAD * ml_heads
    da_v_dim = da_subln_g.shape[1]
    da_qk_dim = da_lambda.shape[2]
    da_width = w_branch_b.shape[1]
    da_heads = da_width // da_v_dim
    assert ml_width == d and da_width == d and da_v_dim == LANES and 2 * da_qk_dim == LANES
    assert w_in.shape[2] == 11 * d + n_gates and ml_gate_b.shape[1] == GATES_PER_HEAD
    n = batch * seq

    col = {name: i for i, name in enumerate(
        ("a_q", "a_k", "a_o", "a_z", "b_q", "b_k", "b_z", "g_a", "g_b"))}
    bv0 = 7 * d + n_gates

    tiles = _tiles(seq)
    tm_proj, tm_prep, tm_merge = tiles["in_proj"], tiles["prep"], tiles["merge"]
    tq, tk = tiles["attn_q"], tiles["attn_k"]

    x2 = x.reshape(n, d)
    pos2 = positions.reshape(n, 1)
    inv_freq = ROPE_THETA ** (-jnp.arange(0, da_qk_dim, 2, dtype=F32) / da_qk_dim)
    freq_tile = jnp.tile(inv_freq, LANES // inv_freq.shape[0]).reshape(1, LANES)

    for layer in range(depth):
        w = w_in[layer]
        w_qk = w[:, 5 * d + n_gates:bv0].reshape(d, 2 * da_heads, 2, 2, da_qk_dim // 2)
        w_qk = w_qk.transpose(0, 1, 3, 2, 4).reshape(d, 2 * da_width)
        w_main = jnp.concatenate([w[:, :2 * d], w[:, 3 * d:5 * d], w_qk, w[:, bv0 + d:]],
                                 axis=1).astype(BF16)
        wvt = w[:, bv0:bv0 + d].astype(BF16)
        wavt = w[:, 2 * d:3 * d].astype(BF16)
        w_g = w[:, 5 * d:5 * d + n_gates].reshape(d, GATES_PER_HEAD, ml_heads).transpose(0, 2, 1)
        w_g = w_g.reshape(d, n_gates).astype(BF16)
        wg = jnp.pad(w_g, ((0, 0), (0, LANES - n_gates)))
        bias_hg = ml_gate_b[layer].astype(F32).T
        bias_row = jnp.pad(bias_hg.reshape(1, n_gates), ((0, 0), (0, LANES - n_gates)))
        bias_col = bias_hg.reshape(n_gates, 1)

        p, gates, gates_t, vt, avt = _in_proj(
            x2, norm_g[layer].reshape(1, d), w_main, wg, w_g.T, wvt, wavt, pos2, freq_tile,
            tm=tm_proj, tn=d, tk=tk, rope_cols=(col["b_q"], col["b_k"]),
            q_scale=da_qk_dim ** -0.5 * math.log2(math.e))
        qk = _prep(p, ml_conv_w[layer].astype(F32), seq=seq, ml_width=ml_width, tm=tm_prep,
                   k_scale=ml_head_dim ** -0.5)
        h_fw, h_bw = _mlstm(qk, avt, gates, gates_t, bias_row, bias_col,
                            batch=batch, seq=seq, heads=ml_heads, head_dim=ml_head_dim)
        lambda_init = 0.8 - 0.6 * math.exp(-0.3 * layer)
        y_b = _diff_attn(da_lambda[layer].astype(F32), da_subln_g[layer].reshape(1, LANES), vt, p,
                         batch=batch, seq=seq, heads=da_heads, q_col=col["b_q"] * da_heads,
                         k_col=col["b_k"] * da_heads, z_col=col["b_z"] * da_heads, tq=tq,
                         lambda_init=lambda_init)
        x2 = _merge(h_fw, h_bw, p, y_b, x2, ml_norm_g[layer].reshape(1, ml_width), gate_b[layer],
                    final_g.reshape(1, d), w_branch_a[layer].astype(BF16),
                    w_branch_b[layer].astype(BF16), w_out[layer].astype(BF16), tm=tm_merge,
                    head_dim=ml_head_dim, o_col=col["a_o"], z_col=col["a_z"], ga_col=col["g_a"],
                    gb_col=col["g_b"], final_norm=(layer == depth - 1))
    return x2.reshape(batch, seq, d)
```
